```python
import math
import jax, jax.numpy as jnp
from jax import lax
import numpy as np

D_MODEL = 1024
BATCH = 2
SEQ = 8192
DEPTH = 1

DA_QK_DIM = 64
DA_V_DIM = 2 * DA_QK_DIM
DA_WIDTH = D_MODEL // 2
DA_HEADS = DA_WIDTH // DA_V_DIM
DA_QK_COLS = 2 * DA_HEADS * DA_QK_DIM
ROPE_DIM = DA_QK_DIM // 4
ROPE_THETA = 500000.0
Q_BLOCK = 128
ML_WIDTH = D_MODEL // 2
ML_HEADS = 4
ML_HEAD_DIM = ML_WIDTH // ML_HEADS
ML_CONV = 5
ML_CHUNK = 64
STAB_INIT = -1e30
N_BRANCH = 2
IN_SPLIT_SIZES = (DA_QK_COLS, DA_QK_COLS, DA_WIDTH,
                  ML_WIDTH, ML_WIDTH, ML_WIDTH, ML_WIDTH, 4 * ML_HEADS,
                  N_BRANCH * D_MODEL)
IN_COLS = sum(IN_SPLIT_SIZES)
N_KEYS = 128
N_EXPERTS = N_KEYS * N_KEYS
PEER_HEADS = 8
PEER_QDIM = 256
PEER_TOPK = 16
PEER_TOK_BLOCK = 128
DN_ALPHA = (2 * DEPTH) ** 0.25
DN_BETA = (8 * DEPTH) ** -0.25
LN_EPS = 1e-5

kernel_name = "hybrid_diffattn_mlstm_peer_encoder"


def layer_norm(x, g, b):
    xf = x.astype(jnp.float32)
    mu = xf.mean(-1, keepdims=True)
    var = jnp.square(xf - mu).mean(-1, keepdims=True)
    return ((xf - mu) * lax.rsqrt(var + LN_EPS) * g.astype(jnp.float32)
            + b.astype(jnp.float32)).astype(x.dtype)


def rms_norm(x, g):
    xf = x.astype(jnp.float32)
    y = xf * lax.rsqrt(jnp.mean(xf * xf, -1, keepdims=True) + LN_EPS)
    return (y * g.astype(jnp.float32)).astype(x.dtype)


def head_layer_norm(x, g):
    xf = x.astype(jnp.float32)
    mu = xf.mean(-1, keepdims=True)
    var = jnp.square(xf - mu).mean(-1, keepdims=True)
    y = (xf - mu) * lax.rsqrt(var + LN_EPS)
    return y.reshape(x.shape[:-2] + (-1,)) * g.astype(jnp.float32)


def partial_rope(t, cos, sin):
    half = ROPE_DIM // 2
    c = cos[None, :, None, None, :]
    s = sin[None, :, None, None, :]
    t1 = t[..., :half].astype(jnp.float32)
    t2 = t[..., half:ROPE_DIM].astype(jnp.float32)
    rot = jnp.concatenate([t1 * c - t2 * s, t2 * c + t1 * s], -1).astype(t.dtype)
    return jnp.concatenate([rot, t[..., ROPE_DIM:]], -1)


def diff_attention(q, k, v, lam_params, subln_g, lambda_init):
    B, S, H, _, Dq = q.shape
    lp = lam_params.astype(jnp.float32)
    lam = jnp.exp(jnp.sum(lp[0] * lp[1])) - jnp.exp(jnp.sum(lp[2] * lp[3])) + lambda_init
    nb = S // Q_BLOCK
    qb = (q * (Dq ** -0.5)).reshape(B, nb, Q_BLOCK, H, 2, Dq).transpose(1, 0, 3, 4, 2, 5)
    kh = k.transpose(0, 2, 3, 1, 4)
    vh = v.transpose(0, 2, 1, 3)

    def block(qblk):
        s = jnp.einsum('bhmqd,bhmkd->bhmqk', qblk, kh,
                       preferred_element_type=jnp.float32)
        p = jax.nn.softmax(s, axis=-1)
        a = p[:, :, 0] - lam * p[:, :, 1]
        return jnp.einsum('bhqk,bhkd->bhqd', a.astype(vh.dtype), vh)

    o = lax.map(block, qb)
    o = o.transpose(1, 0, 3, 2, 4).reshape(B, S, H, -1)
    o = rms_norm(o, subln_g) * (1.0 - lambda_init)
    return o.reshape(B, S, -1)


def mlstm_chunkwise(q, k, v, i_pre, log_f):
    B, H, S, D = q.shape
    nc = S // ML_CHUNK
    to_chunks = lambda t: jnp.moveaxis(t.reshape(B, H, nc, ML_CHUNK, *t.shape[3:]), 2, 0)
    qc, kc, vc = to_chunks(q), to_chunks(k), to_chunks(v)
    ic = to_chunks(i_pre)
    bc = jnp.cumsum(to_chunks(log_f), axis=-1)
    causal_in_chunk = jnp.tril(jnp.ones((ML_CHUNK, ML_CHUNK), bool))

    def step(carry, inp):
        C, n, m = carry
        qj, kj, vj, ij, bj = inp
        bL = bj[..., -1]
        dmat = jnp.where(causal_in_chunk, bj[..., :, None] - bj[..., None, :] + ij[..., None, :], -jnp.inf)
        inter = bj + m[..., None]
        m_row = jnp.maximum(inter, dmat.max(-1))
        w_inter = jnp.exp(inter - m_row)
        s = jnp.einsum('bhjd,bhld->bhjl', qj, kj) * jnp.exp(dmat - m_row[..., None])
        num = w_inter[..., None] * jnp.einsum('bhjd,bhde->bhje', qj, C) + jnp.einsum('bhjl,bhle->bhje', s, vj)
        den = w_inter * jnp.einsum('bhjd,bhd->bhj', qj, n) + s.sum(-1)
        h = num / jnp.maximum(jnp.abs(den), jnp.exp(-m_row))[..., None]
        g = bL[..., None] - bj + ij
        m_new = jnp.maximum(bL + m, g.max(-1))
        a = jnp.exp(bL + m - m_new)
        wk = jnp.exp(g - m_new[..., None])
        C = a[..., None, None] * C + jnp.einsum('bhl,bhld,bhle->bhde', wk, kj, vj)
        n = a[..., None] * n + jnp.einsum('bhl,bhld->bhd', wk, kj)
        return (C, n, m_new), h

    init = (jnp.zeros((B, H, D, D), jnp.float32), jnp.zeros((B, H, D), jnp.float32),
            jnp.full((B, H), STAB_INIT, jnp.float32))
    _, hs = lax.scan(step, init, (qc, kc, vc, ic, bc))
    return jnp.moveaxis(hs, 0, 2).reshape(B, H, S, D)


def depthwise_conv_centred(x, w, b):
    pad = (w.shape[0] - 1) // 2
    y = lax.conv_general_dilated(x, w[:, None, :].astype(x.dtype), window_strides=(1,),
                                 padding=[(pad, pad)], dimension_numbers=('NWC', 'WIO', 'NWC'),
                                 feature_group_count=x.shape[-1])
    return y + b


def mlstm_branch(ml_q, ml_k, ml_v, ml_o, ml_g, conv_w, conv_b, gate_b, norm_g):
    B, S, _ = ml_q.shape
    qk = jax.nn.silu(depthwise_conv_centred(jnp.concatenate([ml_q, ml_k], -1), conv_w, conv_b))
    q, k = jnp.split(qk.astype(jnp.float32), 2, axis=-1)
    heads = lambda t: t.reshape(B, S, ML_HEADS, ML_HEAD_DIM).transpose(0, 2, 1, 3)
    q, k, v = heads(q), heads(k) * (ML_HEAD_DIM ** -0.5), heads(ml_v.astype(jnp.float32))
    gates = (ml_g.astype(jnp.float32).reshape(B, S, 4, ML_HEADS)
             + gate_b.astype(jnp.float32)).transpose(2, 0, 3, 1)
    i_fw, f_fw, i_bw, f_bw = gates[0], gates[1], gates[2], gates[3]
    h_fw = mlstm_chunkwise(q, k, v, i_fw, jax.nn.log_sigmoid(f_fw))
    flip = lambda t: jnp.flip(t, axis=2)
    h_bw = flip(mlstm_chunkwise(flip(q), flip(k), flip(v), flip(i_bw), flip(jax.nn.log_sigmoid(f_bw))))
    h = (h_fw + h_bw).transpose(0, 2, 1, 3)
    h = head_layer_norm(h, norm_g) * jax.nn.sigmoid(ml_o.astype(jnp.float32))
    return h.astype(ml_q.dtype)


def peer(x, w_q, sub_keys, u_tab, v_tab):
    B, S, D = x.shape
    T = B * S
    xt = x.reshape(T, D)
    q = (xt @ w_q).reshape(T, PEER_HEADS, 2, PEER_QDIM // 2)
    s = jnp.einsum('thpd,pnd->thpn', q, sub_keys, preferred_element_type=jnp.float32)
    sc, idx = lax.top_k(s, PEER_TOPK)
    cand = (sc[:, :, 0, :, None] + sc[:, :, 1, None, :]).reshape(T, PEER_HEADS, -1)
    cand_idx = (idx[:, :, 0, :, None] * N_KEYS + idx[:, :, 1, None, :]).reshape(T, PEER_HEADS, -1)
    top_sc, top_pos = lax.top_k(cand, PEER_TOPK)
    experts = jnp.take_along_axis(cand_idx, top_pos, axis=-1)
    gates = jax.nn.softmax(top_sc, axis=-1)
    nb = T // PEER_TOK_BLOCK

    def block(args):
        xb, eb, gb = args
        act = jax.nn.gelu(jnp.einsum('ted,td->te', u_tab[eb], xb), approximate=False)
        return jnp.einsum('te,ted->td', gb * act, v_tab[eb])

    y = lax.map(block, (xt.reshape(nb, PEER_TOK_BLOCK, D),
                        experts.reshape(nb, PEER_TOK_BLOCK, -1),
                        gates.reshape(nb, PEER_TOK_BLOCK, -1).astype(x.dtype)))
    return y.reshape(B, S, D)


def setup_inputs(seed: int = 0) -> dict:
    key = jax.random.key(seed)
    ks = jax.random.split(key, 20)
    L, D = DEPTH, D_MODEL
    nrm = lambda k, shape, scale: jax.random.normal(k, shape, jnp.float32) * scale
    fbias = jnp.linspace(3.0, 6.0, ML_HEADS, dtype=jnp.float32)
    zb = jnp.zeros((ML_HEADS,), jnp.float32)
    gate_base = jnp.stack([zb, fbias, zb, fbias])[None]
    return {
        "x": nrm(ks[0], (BATCH, SEQ, D), 1.0),
        "w_in": nrm(ks[1], (L, D, IN_COLS), D ** -0.5),
        "da_lambda": nrm(ks[2], (L, 4, DA_QK_DIM), 0.1),
        "da_subln_g": 1.0 + nrm(ks[3], (L, DA_V_DIM), 0.02),
        "ml_conv_w": nrm(ks[4], (L, ML_CONV, 2 * ML_WIDTH), ML_CONV ** -0.5),
        "ml_conv_b": nrm(ks[5], (L, 2 * ML_WIDTH), 0.02),
        "ml_gate_b": gate_base + nrm(ks[6], (L, 4, ML_HEADS), 0.1),
        "ml_norm_g": 1.0 + nrm(ks[7], (L, ML_WIDTH), 0.02),
        "w_branch_attn": nrm(ks[8], (L, DA_WIDTH, D), DA_WIDTH ** -0.5),
        "w_branch_mlstm": nrm(ks[9], (L, ML_WIDTH, D), ML_WIDTH ** -0.5),
        "w_out": nrm(ks[10], (L, D, D), DN_BETA * D ** -0.5),
        "ln1_g": 1.0 + nrm(ks[11], (L, D), 0.02),
        "ln1_b": nrm(ks[12], (L, D), 0.02),
        "peer_w_q": nrm(ks[13], (L, D, PEER_HEADS * PEER_QDIM), D ** -0.5),
        "peer_sub_keys": nrm(ks[14], (L, 2, N_KEYS, PEER_QDIM // 2), (PEER_QDIM // 2) ** -0.5),
        "peer_u": nrm(ks[15], (L, N_EXPERTS, D), D ** -0.5),
        "peer_v": nrm(ks[16], (L, N_EXPERTS, D), DN_BETA),
        "ln2_g": 1.0 + nrm(ks[17], (L, D), 0.02),
        "ln2_b": nrm(ks[18], (L, D), 0.02),
    }


def reference(x, w_in, da_lambda, da_subln_g, ml_conv_w, ml_conv_b, ml_gate_b, ml_norm_g,
              w_branch_attn, w_branch_mlstm, w_out, ln1_g, ln1_b,
              peer_w_q, peer_sub_keys, peer_u, peer_v, ln2_g, ln2_b):
    B, S, D = x.shape
    split_at = [int(c) for c in np.cumsum(IN_SPLIT_SIZES)[:-1]]
    pos = jnp.arange(S, dtype=jnp.float32)
    inv_freq = ROPE_THETA ** (-jnp.arange(0, ROPE_DIM, 2, dtype=jnp.float32) / ROPE_DIM)
    ang = pos[:, None] * inv_freq[None, :]
    cos, sin = jnp.cos(ang), jnp.sin(ang)
    for l in range(DEPTH):
        lambda_init = 0.8 - 0.6 * math.exp(-0.3 * l)
        proj = x @ w_in[l]
        da_q, da_k, da_v, ml_q, ml_k, ml_v, ml_o, ml_g, br_g = jnp.split(proj, split_at, axis=-1)
        qa = partial_rope(da_q.reshape(B, S, DA_HEADS, 2, DA_QK_DIM), cos, sin)
        ka = partial_rope(da_k.reshape(B, S, DA_HEADS, 2, DA_QK_DIM), cos, sin)
        va = da_v.reshape(B, S, DA_HEADS, DA_V_DIM)
        y_attn = diff_attention(qa, ka, va, da_lambda[l], da_subln_g[l], lambda_init)
        y_mlstm = mlstm_branch(ml_q, ml_k, ml_v, ml_o, ml_g, ml_conv_w[l], ml_conv_b[l],
                               ml_gate_b[l], ml_norm_g[l])
        gates = jax.nn.sigmoid(br_g.reshape(B, S, N_BRANCH, D))
        merged = gates[:, :, 0] * (y_attn @ w_branch_attn[l]) + gates[:, :, 1] * (y_mlstm @ w_branch_mlstm[l])
        x = layer_norm(DN_ALPHA * x + (merged @ w_out[l]).astype(x.dtype), ln1_g[l], ln1_b[l])
        y_peer = peer(x, peer_w_q[l], peer_sub_keys[l], peer_u[l], peer_v[l])
        x = layer_norm(DN_ALPHA * x + y_peer.astype(x.dtype), ln2_g[l], ln2_b[l])
    return x
```

```python
import functools
import math

import jax
import jax.numpy as jnp
from jax import lax
from jax.experimental import pallas as pl
from jax.experimental.pallas import tpu as pltpu

D_MODEL = 1024
DA_QK_DIM = 64
DA_V_DIM = 128
DA_HEADS = 4
DA_WIDTH = 512
ROPE_DIM = 16
ROPE_THETA = 500000.0
ML_WIDTH = 512
ML_HEADS = 4
ML_HEAD_DIM = 128
ML_CONV = 5
ML_CHUNK = 64
STAB_INIT = -1e30
N_KEYS = 128
N_EXPERTS = N_KEYS * N_KEYS
PEER_HEADS = 8
PEER_TOPK = 16
DEPTH = 1
DN_ALPHA = (2 * DEPTH) ** 0.25
LN_EPS = 1e-5
LAMBDA_INIT = 0.8 - 0.6 * math.exp(-0.3 * 0)

LANES = 128
NEG_INF = float("-inf")
POS_INF = float("inf")
VMEM_LIMIT = 48 * 1024 * 1024

COL_BR = 0
COL_Q, COL_K, COL_V = 2048, 2560, 3072
COL_MLQ, COL_MLK, COL_MLV, COL_MLO = 3584, 4096, 4608, 5120
COL_MLG = 5632
PROJ_COLS = COL_MLG + LANES
N_GATE_COLS = 4 * ML_HEADS
IN_MAIN = 3584


def _params(*sem):
    return pltpu.CompilerParams(dimension_semantics=sem, vmem_limit_bytes=VMEM_LIMIT)


def _mm_kernel(x_ref, w_ref, o_ref):
    o_ref[...] = jnp.dot(x_ref[...].astype(jnp.bfloat16), w_ref[...],
                         preferred_element_type=jnp.float32).astype(o_ref.dtype)


def _matmul(x, w, tm, tn, out_dtype=jnp.float32):
    m, k = x.shape
    n = w.shape[1]
    tm, tn = min(tm, m), min(tn, n)
    return pl.pallas_call(
        _mm_kernel, grid=(m // tm, n // tn),
        in_specs=[pl.BlockSpec((tm, k), lambda i, j: (i, 0)),
                  pl.BlockSpec((k, tn), lambda i, j: (0, j))],
        out_specs=pl.BlockSpec((tm, tn), lambda i, j: (i, j)),
        out_shape=jax.ShapeDtypeStruct((m, n), out_dtype),
        compiler_params=_params("parallel", "parallel"), name="dense_matmul")(x, w)


def _split_bf16(a):
    hi = a.astype(jnp.bfloat16)
    lo = (a - hi.astype(jnp.float32)).astype(jnp.bfloat16)
    return hi, lo


def _dot3(a_hi, a_lo, b_hi, b_lo, dims):
    f = lambda a, b: lax.dot_general(a, b, (dims, ((), ())), preferred_element_type=jnp.float32)
    return f(a_hi, b_hi) + f(a_hi, b_lo) + f(a_lo, b_hi)


def _mm3_kernel(x_ref, wh_ref, wl_ref, o_ref):
    xh, xl = _split_bf16(x_ref[...])
    o_ref[...] = _dot3(xh, xl, wh_ref[...], wl_ref[...], ((1,), (0,)))


def _matmul3(x, w_hi, w_lo, tm, tn):
    m, k = x.shape
    n = w_hi.shape[1]
    tm, tn = min(tm, m), min(tn, n)
    return pl.pallas_call(
        _mm3_kernel, grid=(m // tm, n // tn),
        in_specs=[pl.BlockSpec((tm, k), lambda i, j: (i, 0)),
                  pl.BlockSpec((k, tn), lambda i, j: (0, j)),
                  pl.BlockSpec((k, tn), lambda i, j: (0, j))],
        out_specs=pl.BlockSpec((tm, tn), lambda i, j: (i, j)),
        out_shape=jax.ShapeDtypeStruct((m, n), jnp.float32),
        compiler_params=_params("parallel", "parallel"), name="dense_matmul3")(x, w_hi, w_lo)


def _rope_kernel(q_ref, k_ref, v_ref, c_ref, s1_ref, s2_ref, o_ref):
    c, s1, s2 = c_ref[...], s1_ref[...], s2_ref[...]
    for which, src in enumerate((q_ref, k_ref)):
        for hd in range(DA_HEADS):
            x = src[:, hd * LANES:(hd + 1) * LANES]
            r = x * c + pltpu.roll(x, LANES - 8, 1) * s1 + pltpu.roll(x, 8, 1) * s2
            if which == 0:
                r = r * (DA_QK_DIM ** -0.5)
            col = which * DA_WIDTH + hd * LANES
            o_ref[:, col:col + LANES] = r.astype(jnp.bfloat16)
    o_ref[:, 2 * DA_WIDTH:3 * DA_WIDTH] = v_ref[...].astype(jnp.bfloat16)


def _rope_tables(seq):
    pos = jnp.arange(seq, dtype=jnp.float32)
    inv_freq = ROPE_THETA ** (-jnp.arange(0, ROPE_DIM, 2, dtype=jnp.float32) / ROPE_DIM)
    ang = pos[:, None] * inv_freq[None, :]
    cos, sin = jnp.cos(ang), jnp.sin(ang)
    half = ROPE_DIM // 2
    zeros = jnp.zeros((seq, DA_QK_DIM - ROPE_DIM), jnp.float32)
    z8 = jnp.zeros((seq, half), jnp.float32)
    c64 = jnp.concatenate([cos, cos, zeros + 1.0], -1)
    s1_64 = jnp.concatenate([-sin, z8, zeros], -1)
    s2_64 = jnp.concatenate([z8, sin, zeros], -1)
    tile2 = lambda t: jnp.concatenate([t, t], -1)
    return tile2(c64), tile2(s1_64), tile2(s2_64)


def _rope_cast(proj, seq, tm):
    t = proj.shape[0]
    tm = min(tm, seq)
    nsb = seq // tm
    c, s1, s2 = _rope_tables(seq)
    tab = pl.BlockSpec((tm, LANES), lambda i: (i % nsb, 0))
    src = lambda col: pl.BlockSpec((tm, DA_WIDTH), lambda i: (i, col // DA_WIDTH))
    return pl.pallas_call(
        _rope_kernel, grid=(t // tm,),
        in_specs=[src(COL_Q), src(COL_K), src(COL_V), tab, tab, tab],
        out_specs=pl.BlockSpec((tm, 3 * DA_WIDTH), lambda i: (i, 0)),
        out_shape=jax.ShapeDtypeStruct((t, 3 * DA_WIDTH), jnp.bfloat16),
        compiler_params=_params("parallel"), name="rope_cast")(proj, proj, proj, c, s1, s2)


def _attn_kernel(lam_ref, g_ref, q_ref, k_ref, v_ref, o_ref, m_ref, acc_ref, *, tk):
    tq = q_ref.shape[0]
    nk = k_ref.shape[0] // tk
    m_ref[...] = jnp.full(m_ref.shape, NEG_INF, jnp.float32)
    acc_ref[...] = jnp.zeros(acc_ref.shape, jnp.float32)
    lane = lax.broadcasted_iota(jnp.int32, (tk, LANES), 1)
    ones_col = jnp.where(lane == 0, 1.0, 0.0).astype(jnp.bfloat16)

    def body(j, carry):
        r0 = pl.multiple_of(j * tk, tk)
        k = k_ref[pl.ds(r0, tk), :]
        vext = jnp.concatenate([v_ref[pl.ds(r0, tk), :], ones_col], axis=1)
        for mp in range(2):
            q = q_ref[:, mp * DA_QK_DIM:(mp + 1) * DA_QK_DIM]
            s = lax.dot_general(q, k[:, mp * DA_QK_DIM:(mp + 1) * DA_QK_DIM],
                                (((1,), (1,)), ((), ())), preferred_element_type=jnp.float32)
            m_old = m_ref[mp]
            m_new = jnp.maximum(m_old, jnp.max(s, axis=-1, keepdims=True))
            p = jnp.exp(s - m_new).astype(jnp.bfloat16)
            pv = jnp.dot(p, vext, preferred_element_type=jnp.float32)
            acc_ref[mp] = jnp.exp(m_old - m_new) * acc_ref[mp] + pv
            m_ref[mp] = m_new
        return carry

    lax.fori_loop(0, nk, body, 0)

    lp = lam_ref[...]
    lam = (jnp.exp(jnp.sum(lp[0:1] * lp[1:2], axis=-1, keepdims=True))
           - jnp.exp(jnp.sum(lp[2:3] * lp[3:4], axis=-1, keepdims=True)) + LAMBDA_INIT)
    a0, a1 = acc_ref[0], acc_ref[1]
    o = (a0[:, :DA_V_DIM] / a0[:, DA_V_DIM:DA_V_DIM + 1]
         - lam * (a1[:, :DA_V_DIM] / a1[:, DA_V_DIM:DA_V_DIM + 1]))
    y = o * lax.rsqrt(jnp.mean(o * o, axis=-1, keepdims=True) + LN_EPS)
    o_ref[...] = y * g_ref[...] * (1.0 - LAMBDA_INIT)


def _diff_attention(qkv, da_lambda, subln_g, batch, seq, tq, tk):
    t = qkv.shape[0]
    tq, tk = min(tq, seq), min(tk, seq)
    nq = seq // tq
    kern = functools.partial(_attn_kernel, tk=tk)
    return pl.pallas_call(
        kern, grid=(batch, DA_HEADS, nq),
        in_specs=[pl.BlockSpec((4, DA_QK_DIM), lambda b, h, i: (0, 0)),
                  pl.BlockSpec((1, DA_V_DIM), lambda b, h, i: (0, 0)),
                  pl.BlockSpec((tq, LANES), lambda b, h, i: (b * nq + i, h)),
                  pl.BlockSpec((seq, LANES), lambda b, h, i: (b, DA_HEADS + h)),
                  pl.BlockSpec((seq, LANES), lambda b, h, i: (b, 2 * DA_HEADS + h))],
        out_specs=pl.BlockSpec((tq, LANES), lambda b, h, i: (b * nq + i, h)),
        out_shape=jax.ShapeDtypeStruct((t, DA_WIDTH), jnp.float32),
        scratch_shapes=[pltpu.VMEM((2, tq, 1), jnp.float32),
                        pltpu.VMEM((2, tq, 2 * LANES), jnp.float32)],
        compiler_params=_params("parallel", "parallel", "parallel"),
        name="diff_attention")(da_lambda, subln_g.reshape(1, DA_V_DIM), qkv, qkv, qkv)


def _conv_kernel(prev_ref, cur_ref, next_ref, w_ref, b_ref, o_ref, *, nsb):
    i, c = pl.program_id(0), pl.program_id(1)
    tm = cur_ref.shape[0]
    sb = i % nsb
    prev = jnp.where(sb == 0, 0.0, prev_ref[...])
    nxt = jnp.where(sb == nsb - 1, 0.0, next_ref[...])
    ext = jnp.concatenate([prev, cur_ref[...], nxt], axis=0)
    n = tm + 16
    pad = (ML_CONV - 1) // 2
    acc = jnp.zeros((tm, cur_ref.shape[1]), jnp.float32) + b_ref[...]
    for w in range(ML_CONV):
        shift = (pad - w) % n
        z = ext if shift == 0 else pltpu.roll(ext, shift, 0)
        acc = acc + z[8:8 + tm] * w_ref[w:w + 1, :]
    y = acc * jax.nn.sigmoid(acc)
    y = y * jnp.where(c == 1, ML_HEAD_DIM ** -0.5, 1.0)
    o_ref[...] = y.astype(jnp.bfloat16)


def _conv_silu(proj, conv_w, conv_b, seq, tm):
    t = proj.shape[0]
    tm = min(tm, seq)
    nsb = seq // tm
    r8 = tm // 8
    nrow8 = t // 8
    cb0 = COL_MLQ // ML_WIDTH
    kern = functools.partial(_conv_kernel, nsb=nsb)
    return pl.pallas_call(
        kern, grid=(t // tm, 2),
        in_specs=[pl.BlockSpec((8, ML_WIDTH), lambda i, c: (jnp.maximum(i * r8 - 1, 0), cb0 + c)),
                  pl.BlockSpec((tm, ML_WIDTH), lambda i, c: (i, cb0 + c)),
                  pl.BlockSpec((8, ML_WIDTH), lambda i, c: (jnp.minimum((i + 1) * r8, nrow8 - 1), cb0 + c)),
                  pl.BlockSpec((ML_CONV, ML_WIDTH), lambda i, c: (0, c)),
                  pl.BlockSpec((1, ML_WIDTH), lambda i, c: (0, c))],
        out_specs=pl.BlockSpec((tm, ML_WIDTH), lambda i, c: (i, c)),
        out_shape=jax.ShapeDtypeStruct((t, 2 * ML_WIDTH), jnp.bfloat16),
        compiler_params=_params("parallel", "parallel"),
        name="conv_silu")(proj, proj, proj, conv_w, conv_b.reshape(1, 2 * ML_WIDTH))


def _seg_scan(x, pos, seq, op, fill, reverse):
    y = x
    d = 1
    while d < ML_CHUNK:
        if reverse:
            y = op(y, jnp.where(pos < ML_CHUNK - d, pltpu.roll(y, seq - d, 1), fill))
        else:
            y = op(y, jnp.where(pos >= d, pltpu.roll(y, d, 1), fill))
        d *= 2
    return y


def _gate_dir(i_pre, logf, pos, lane, seq, reverse):
    bj = _seg_scan(logf, pos, seq, jnp.add, 0.0, reverse)
    last = (pos == 0) if reverse else (pos == ML_CHUNK - 1)
    b_tot = _seg_scan(jnp.where(last, bj, NEG_INF), pos, seq, jnp.maximum, NEG_INF, not reverse)
    w = i_pre - bj
    cmax_w = _seg_scan(w, pos, seq, jnp.maximum, NEG_INF, reverse)
    g = b_tot - bj + i_pre
    g_max = jnp.maximum(_seg_scan(g, pos, seq, jnp.maximum, NEG_INF, False),
                        _seg_scan(g, pos, seq, jnp.maximum, NEG_INF, True))
    acc_a, acc_g = b_tot, g_max
    d = ML_CHUNK
    while d < seq:
        if reverse:
            ok = lane < seq - d
            pa = jnp.where(ok, pltpu.roll(acc_a, seq - d, 1), 0.0)
            pg = jnp.where(ok, pltpu.roll(acc_g, seq - d, 1), NEG_INF)
        else:
            ok = lane >= d
            pa = jnp.where(ok, pltpu.roll(acc_a, d, 1), 0.0)
            pg = jnp.where(ok, pltpu.roll(acc_g, d, 1), NEG_INF)
        acc_g = jnp.maximum(pg + acc_a, acc_g)
        acc_a = pa + acc_a
        d *= 2
    m_after = jnp.maximum(acc_a + STAB_INIT, acc_g)
    if reverse:
        m_before = jnp.where(lane < seq - ML_CHUNK, pltpu.roll(m_after, seq - ML_CHUNK, 1), STAB_INIT)
    else:
        m_before = jnp.where(lane >= ML_CHUNK, pltpu.roll(m_after, ML_CHUNK, 1), STAB_INIT)
    mr_rel = jnp.maximum(m_before, cmax_w)
    m_row = bj + mr_rel
    u = -mr_rel
    w_inter = jnp.exp(m_before - mr_rel)
    e_negm = jnp.exp(-m_row)
    wk = jnp.exp(g - m_after)
    a = jnp.exp(b_tot + m_before - m_after)
    return u, w, w_inter, e_negm, wk, a


def _gate_kernel(g_ref, b_ref, o_ref):
    seq = g_ref.shape[2]
    g = g_ref[0] + b_ref[...]
    i_pre, f_pre = g[0:8], g[8:16]
    logf = jnp.minimum(f_pre, 0.0) - jnp.log1p(jnp.exp(-jnp.abs(f_pre)))
    lane = lax.broadcasted_iota(jnp.int32, (2 * ML_HEADS, seq), 1)
    row = lax.broadcasted_iota(jnp.int32, (2 * ML_HEADS, seq), 0)
    pos = lane % ML_CHUNK
    fwd = _gate_dir(i_pre, logf, pos, lane, seq, False)
    bwd = _gate_dir(i_pre, logf, pos, lane, seq, True)
    for q, (af, ab) in enumerate(zip(fwd, bwd)):
        o_ref[0, q] = jnp.where(row < ML_HEADS, af, ab)


def _gate_scans(gates_t, gate_b):
    b, _, seq = gates_t.shape
    return pl.pallas_call(
        _gate_kernel, grid=(b,),
        in_specs=[pl.BlockSpec((1, 16, seq), lambda i: (i, 0, 0)),
                  pl.BlockSpec((16, 1), lambda i: (0, 0))],
        out_specs=pl.BlockSpec((1, 6, 8, seq), lambda i: (i, 0, 0, 0)),
        out_shape=jax.ShapeDtypeStruct((b, 6, 8, seq), jnp.float32),
        compiler_params=_params("parallel"), name="mlstm_gate_scans")(gates_t, gate_b.reshape(16, 1))


Q_U, Q_WINTER, Q_ENEGM, Q_WK, Q_A = 0, 1, 2, 3, 4


def _mlstm_kernel(qf_ref, kf_ref, vf_ref, cf_ref, wf_ref, qb_ref, kb_ref, vb_ref, cb_ref, wb_ref,
                  of_ref, ob_ref, ce_ref, *, chunks):
    @pl.when(pl.program_id(1) == 0)
    def _():
        ce_ref[...] = jnp.zeros(ce_ref.shape, jnp.float32)

    L = ML_CHUNK
    row = lax.broadcasted_iota(jnp.int32, (L, L), 0)
    col = lax.broadcasted_iota(jnp.int32, (L, L), 1)
    lane = lax.broadcasted_iota(jnp.int32, (L, LANES), 1)
    ones_col = jnp.where(lane == 0, 1.0, 0.0).astype(jnp.bfloat16)
    dirs = ((0, qf_ref, kf_ref, vf_ref, cf_ref, wf_ref, of_ref, col <= row, range(chunks)),
            (1, qb_ref, kb_ref, vb_ref, cb_ref, wb_ref, ob_ref, col >= row, range(chunks - 1, -1, -1)))
    for d, q_ref, k_ref, v_ref, c_ref, w_ref, o_ref, mask, order in dirs:
        for c in order:
            r0 = c * L
            for h in range(ML_HEADS):
                hs = slice(h * ML_HEAD_DIM, (h + 1) * ML_HEAD_DIM)
                cidx = lambda qn: c_ref[r0:r0 + L, qn * 8 + d * 4 + h:qn * 8 + d * 4 + h + 1]
                q = q_ref[r0:r0 + L, hs]
                k = k_ref[r0:r0 + L, hs]
                vext = jnp.concatenate([v_ref[r0:r0 + L, hs].astype(jnp.bfloat16), ones_col], axis=1)
                s = lax.dot_general(q, k, (((1,), (1,)), ((), ())), preferred_element_type=jnp.float32)
                w_row = w_ref[0, d * 4 + h:d * 4 + h + 1, r0:r0 + L]
                dec = jnp.where(mask, jnp.exp(cidx(Q_U) + w_row), 0.0)
                sd = (s * dec).astype(jnp.bfloat16)
                ce = ce_ref[d * 4 + h]
                inter = jnp.dot(q, ce.astype(jnp.bfloat16), preferred_element_type=jnp.float32)
                intra = jnp.dot(sd, vext, preferred_element_type=jnp.float32)
                tot = cidx(Q_WINTER) * inter + intra
                den = jnp.maximum(jnp.abs(tot[:, ML_HEAD_DIM:ML_HEAD_DIM + 1]), cidx(Q_ENEGM))
                o_ref[r0:r0 + L, hs] = tot[:, :ML_HEAD_DIM] / den
                kw_t = (k.astype(jnp.float32) * cidx(Q_WK)).T.astype(jnp.bfloat16)
                upd = jnp.dot(kw_t, vext, preferred_element_type=jnp.float32)
                a = c_ref[r0:r0 + 1, Q_A * 8 + d * 4 + h:Q_A * 8 + d * 4 + h + 1]
                ce_ref[d * 4 + h] = a * ce + upd


def _mlstm_scan(qk_conv, proj, col_tab, w_tab, batch, seq, chunks):
    t = qk_conv.shape[0]
    rows = chunks * ML_CHUNK
    ns = seq // rows
    vcb = COL_MLV // ML_WIDTH
    ncol = col_tab.shape[1]
    fw = lambda b, j: b * ns + j
    bw = lambda b, j: b * ns + ns - 1 - j
    def specs(rb, lb):
        return [pl.BlockSpec((rows, ML_WIDTH), lambda b, j: (rb(b, j), 0)),
                pl.BlockSpec((rows, ML_WIDTH), lambda b, j: (rb(b, j), 1)),
                pl.BlockSpec((rows, ML_WIDTH), lambda b, j: (rb(b, j), vcb)),
                pl.BlockSpec((rows, ncol), lambda b, j: (rb(b, j), 0)),
                pl.BlockSpec((1, 8, rows), lambda b, j: (b, 0, lb(j)))]
    out_f = pl.BlockSpec((rows, ML_WIDTH), lambda b, j: (fw(b, j), 0))
    out_b = pl.BlockSpec((rows, ML_WIDTH), lambda b, j: (bw(b, j), 0))
    kern = functools.partial(_mlstm_kernel, chunks=chunks)
    return pl.pallas_call(
        kern, grid=(batch, ns),
        in_specs=specs(fw, lambda j: j) + specs(bw, lambda j: ns - 1 - j),
        out_specs=[out_f, out_b],
        out_shape=[jax.ShapeDtypeStruct((t, ML_WIDTH), jnp.float32)] * 2,
        scratch_shapes=[pltpu.VMEM((2 * ML_HEADS, ML_HEAD_DIM, 2 * LANES), jnp.float32)],
        compiler_params=_params("parallel", "arbitrary"),
        name="mlstm_scan")(qk_conv, qk_conv, proj, col_tab, w_tab,
                           qk_conv, qk_conv, proj, col_tab, w_tab)


def _layer_norm_rows(z, g, b):
    mu = jnp.mean(z, axis=-1, keepdims=True)
    zc = z - mu
    var = jnp.mean(zc * zc, axis=-1, keepdims=True)
    return zc * lax.rsqrt(var + LN_EPS) * g + b


def _merge_kernel(x_ref, ya_ref, hf_ref, hb_ref, mo_ref, g0_ref, g1_ref, ng_ref,
                  wa_ref, wm_ref, wo_ref, lg_ref, lb_ref, o_ref):
    h = hf_ref[...] + hb_ref[...]
    parts = []
    for hd in range(ML_HEADS):
        hh = h[:, hd * ML_HEAD_DIM:(hd + 1) * ML_HEAD_DIM]
        mu = jnp.mean(hh, axis=-1, keepdims=True)
        hc = hh - mu
        var = jnp.mean(hc * hc, axis=-1, keepdims=True)
        parts.append(hc * lax.rsqrt(var + LN_EPS))
    ym = jnp.concatenate(parts, axis=1) * ng_ref[...] * jax.nn.sigmoid(mo_ref[...])
    pa = jnp.dot(ya_ref[...].astype(jnp.bfloat16), wa_ref[...], preferred_element_type=jnp.float32)
    pm = jnp.dot(ym.astype(jnp.bfloat16), wm_ref[...], preferred_element_type=jnp.float32)
    merged = jax.nn.sigmoid(g0_ref[...]) * pa + jax.nn.sigmoid(g1_ref[...]) * pm
    mix = jnp.dot(merged.astype(jnp.bfloat16), wo_ref[...], preferred_element_type=jnp.float32)
    o_ref[...] = _layer_norm_rows(DN_ALPHA * x_ref[...] + mix, lg_ref[...], lb_ref[...])


def _merge(xt, y_attn, h_fw, h_bw, proj, norm_g, wa, wm, wo, ln_g, ln_b, tm):
    t = xt.shape[0]
    tm = min(tm, t)
    row = lambda w, cb: pl.BlockSpec((tm, w), lambda i: (i, cb))
    full = lambda a: pl.BlockSpec(a.shape, lambda i: (0, 0))
    vec = lambda a: a.reshape(1, -1)
    args = (xt, y_attn, h_fw, h_bw, proj, proj, proj, vec(norm_g), wa, wm, wo, vec(ln_g), vec(ln_b))
    in_specs = [row(D_MODEL, 0), row(DA_WIDTH, 0), row(ML_WIDTH, 0), row(ML_WIDTH, 0),
                row(ML_WIDTH, COL_MLO // ML_WIDTH),
                row(D_MODEL, COL_BR // D_MODEL), row(D_MODEL, COL_BR // D_MODEL + 1)]
    in_specs += [full(a) for a in args[7:]]
    return pl.pallas_call(
        _merge_kernel, grid=(t // tm,), in_specs=in_specs,
        out_specs=pl.BlockSpec((tm, D_MODEL), lambda i: (i, 0)),
        out_shape=jax.ShapeDtypeStruct((t, D_MODEL), jnp.float32),
        compiler_params=_params("parallel"), name="merge_ln")(*args)


def _topk_rows(s, dst_ref):
    for x in range(PEER_TOPK):
        m = jnp.max(s, axis=0, keepdims=True)
        dst_ref[x:x + 1, :] = m
        s = jnp.where(s == m, NEG_INF, s)


def _peer_prep_kernel(q_ref, kh_ref, kl_ref, s2_ref, c_ref, ea_ref, eb_ref, a_ref, b_ref, sum_ref):
    half = N_KEYS
    sc = []
    for p in range(2):
        qh, ql = _split_bf16(q_ref[:, p * half:(p + 1) * half])
        sc.append(_dot3(kh_ref[p], kl_ref[p], qh, ql, ((1,), (1,))))
    s1, s2 = sc
    _topk_rows(s1, a_ref)
    _topk_rows(s2, b_ref)
    a, b = a_ref[...], b_ref[...]
    for y in range(PEER_TOPK):
        sum_ref[y * PEER_TOPK:(y + 1) * PEER_TOPK, :] = a + b[y:y + 1, :]
    sums = sum_ref[...]
    rest = sums
    for _ in range(PEER_TOPK):
        tau = jnp.max(rest, axis=0, keepdims=True)
        rest = jnp.where(rest == tau, NEG_INF, rest)
    top = a[0:1, :] + b[0:1, :]
    z = jnp.sum(jnp.where(sums >= tau, jnp.exp(sums - top), 0.0), axis=0, keepdims=True)
    cb = jnp.full(a.shape, POS_INF, jnp.float32)
    for y in range(PEER_TOPK):
        by = b[y:y + 1, :]
        cb = jnp.minimum(cb, jnp.where(a + by >= tau, by, POS_INF))
    c = jnp.full(s1.shape, POS_INF, jnp.float32)
    for x in range(PEER_TOPK):
        c = jnp.where(s1 == a[x:x + 1, :], cb[x:x + 1, :], c)
    s2_ref[0] = s2
    c_ref[0] = c
    ea_ref[0] = jnp.exp(s1 - a[0:1, :])
    eb_ref[0] = jnp.exp(s2 - b[0:1, :]) / z


def _peer_prep(q, keys_hi, keys_lo, tt):
    t = q.shape[0]
    tt = min(tt, t)
    qdim = 2 * N_KEYS
    out = jax.ShapeDtypeStruct((PEER_HEADS, N_KEYS, t), jnp.float32)
    ospec = pl.BlockSpec((1, N_KEYS, tt), lambda i, h: (h, 0, i))
    kspec = pl.BlockSpec((2, N_KEYS, N_KEYS), lambda i, h: (0, 0, 0))
    return pl.pallas_call(
        _peer_prep_kernel, grid=(t // tt, PEER_HEADS),
        in_specs=[pl.BlockSpec((tt, qdim), lambda i, h: (i, h)), kspec, kspec],
        out_specs=[ospec] * 4, out_shape=[out] * 4,
        scratch_shapes=[pltpu.VMEM((PEER_TOPK, tt), jnp.float32),
                        pltpu.VMEM((PEER_TOPK, tt), jnp.float32),
                        pltpu.VMEM((PEER_TOPK * PEER_TOPK, tt), jnp.float32)],
        compiler_params=_params("parallel", "parallel"), name="peer_prep")(q, keys_hi, keys_lo)


def _peer_main_kernel(xb_ref, xf_ref, u_ref, vt_ref, s2_ref, c_ref, ea_ref, eb_ref, lg_ref, lb_ref,
                      o_ref, acc_ref, *, rows_per_step):
    e = pl.program_id(1)

    @pl.when(e == 0)
    def _():
        acc_ref[...] = jnp.zeros(acc_ref.shape, jnp.float32)

    pre = jnp.dot(u_ref[...], xb_ref[...], preferred_element_type=jnp.float32)
    act = 0.5 * pre * (1.0 + lax.erf(pre * (2.0 ** -0.5)))
    gparts = []
    for ii in range(rows_per_step):
        i = e * rows_per_step + ii
        g = jnp.zeros((N_KEYS, xb_ref.shape[1]), jnp.float32)
        for h in range(PEER_HEADS):
            thr = c_ref[h, pl.ds(i, 1), :]
            amp = ea_ref[h, pl.ds(i, 1), :]
            g = g + amp * jnp.where(s2_ref[h] >= thr, eb_ref[h], 0.0)
        gparts.append(g)
    gates = jnp.concatenate(gparts, axis=0)
    wgt = (gates * act).astype(jnp.bfloat16)
    acc_ref[...] += jnp.dot(vt_ref[...], wgt, preferred_element_type=jnp.float32)

    @pl.when(e == pl.num_programs(1) - 1)
    def _():
        z = DN_ALPHA * xf_ref[...] + acc_ref[...]
        mu = jnp.mean(z, axis=0, keepdims=True)
        zc = z - mu
        var = jnp.mean(zc * zc, axis=0, keepdims=True)
        o_ref[...] = zc * lax.rsqrt(var + LN_EPS) * lg_ref[...] + lb_ref[...]


def _peer_main(x1t_bf, x1t, u_bf, vt_bf, s2, c, ea, eb, ln_g, ln_b, tb, eb_rows):
    d, t = x1t.shape
    tb = min(tb, t)
    rows_per_step = eb_rows // N_KEYS
    tok = lambda: pl.BlockSpec((d, tb), lambda i, e: (0, i))
    tab = lambda: pl.BlockSpec((PEER_HEADS, N_KEYS, tb), lambda i, e: (0, 0, i))
    colv = lambda: pl.BlockSpec((d, 1), lambda i, e: (0, 0))
    kern = functools.partial(_peer_main_kernel, rows_per_step=rows_per_step)
    return pl.pallas_call(
        kern, grid=(t // tb, N_EXPERTS // eb_rows),
        in_specs=[tok(), tok(),
                  pl.BlockSpec((eb_rows, d), lambda i, e: (e, 0)),
                  pl.BlockSpec((d, eb_rows), lambda i, e: (0, e)),
                  tab(), tab(), tab(), tab(), colv(), colv()],
        out_specs=pl.BlockSpec((d, tb), lambda i, e: (0, i)),
        out_shape=jax.ShapeDtypeStruct((d, t), jnp.float32),
        scratch_shapes=[pltpu.VMEM((d, tb), jnp.float32)],
        compiler_params=_params("parallel", "arbitrary"),
        name="peer_dense")(x1t_bf, x1t, u_bf, vt_bf, s2, c, ea, eb,
                           ln_g.reshape(d, 1), ln_b.reshape(d, 1))


def _token_mixing(xt, batch, seq, w_in, da_lambda, da_subln_g, ml_conv_w, ml_conv_b, ml_gate_b,
                  ml_norm_g, w_branch_attn, w_branch_mlstm, w_out, ln1_g, ln1_b):
    t = batch * seq
    bf = jnp.bfloat16
    zpad = jnp.zeros((D_MODEL, LANES - N_GATE_COLS), jnp.float32)
    w_cols = jnp.concatenate([w_in[:, IN_MAIN + N_GATE_COLS:], w_in[:, :IN_MAIN],
                              w_in[:, IN_MAIN:IN_MAIN + N_GATE_COLS], zpad], axis=1).astype(bf)
    proj = _matmul(xt, w_cols, 512, 1152)

    qkv = _rope_cast(proj, seq, 512)
    y_attn = _diff_attention(qkv, da_lambda, da_subln_g, batch, seq, 512, 512)

    qk_conv = _conv_silu(proj, ml_conv_w, ml_conv_b, seq, 512)
    order = jnp.array([0, 2, 1, 3])
    gates_t = proj[:, COL_MLG:COL_MLG + N_GATE_COLS].reshape(batch, seq, 4, ML_HEADS)[:, :, order]
    gates_t = gates_t.reshape(batch, seq, N_GATE_COLS).transpose(0, 2, 1)
    gtab = _gate_scans(gates_t, ml_gate_b[order].reshape(N_GATE_COLS))
    col_tab = jnp.concatenate([gtab[:, 0], gtab[:, 2], gtab[:, 3], gtab[:, 4], gtab[:, 5]], axis=1)
    col_tab = col_tab.transpose(0, 2, 1).reshape(t, 5 * 8)
    w_tab = gtab[:, 1]
    h_fw, h_bw = _mlstm_scan(qk_conv, proj, col_tab, w_tab, batch, seq,
                             min(4, seq // ML_CHUNK))

    return _merge(xt, y_attn, h_fw, h_bw, proj, ml_norm_g,
                  w_branch_attn.astype(bf), w_branch_mlstm.astype(bf), w_out.astype(bf),
                  ln1_g, ln1_b, 256)


def _peer_layer(x1, peer_w_q, peer_sub_keys, peer_u, peer_v, ln2_g, ln2_b):
    bf = jnp.bfloat16
    wq_hi, wq_lo = _split_bf16(peer_w_q)
    q = _matmul3(x1, wq_hi, wq_lo, 512, 512)
    k_hi, k_lo = _split_bf16(peer_sub_keys)
    s2, c, ea, eb = _peer_prep(q, k_hi, k_lo, 256)
    x1t = x1.T
    out_t = _peer_main(x1t.astype(bf), x1t, peer_u.astype(bf), peer_v.T.astype(bf),
                       s2, c, ea, eb, ln2_g, ln2_b, 512, 512)
    return out_t.T


def kernel(x, w_in, da_lambda, da_subln_g, ml_conv_w, ml_conv_b, ml_gate_b, ml_norm_g,
           w_branch_attn, w_branch_mlstm, w_out, ln1_g, ln1_b,
           peer_w_q, peer_sub_keys, peer_u, peer_v, ln2_g, ln2_b):
    batch, seq, d = x.shape
    xt = x.reshape(batch * seq, d)
    x1 = _token_mixing(xt, batch, seq, w_in[0], da_lambda[0], da_subln_g[0], ml_conv_w[0],
                       ml_conv_b[0], ml_gate_b[0], ml_norm_g[0], w_branch_attn[0],
                       w_branch_mlstm[0], w_out[0], ln1_g[0], ln1_b[0])
    out = _peer_layer(x1, peer_w_q[0], peer_sub_keys[0], peer_u[0], peer_v[0], ln2_g[0], ln2_b[0])
    return out.reshape(batch, seq, d)
```

```python
import functools
import math

import jax
import jax.numpy as jnp
from jax import lax
from jax.experimental import pallas as pl
from jax.experimental.pallas import tpu as pltpu

D_MODEL = 1024
DA_QK_DIM = 64
DA_V_DIM = 128
DA_HEADS = 4
DA_WIDTH = 512
ROPE_DIM = 16
ROPE_THETA = 500000.0
ML_WIDTH = 512
ML_HEADS = 4
ML_HEAD_DIM = 128
ML_CONV = 5
ML_CHUNK = 64
STAB_INIT = -1e30
N_KEYS = 128
N_EXPERTS = N_KEYS * N_KEYS
PEER_HEADS = 8
PEER_TOPK = 16
DEPTH = 1
DN_ALPHA = (2 * DEPTH) ** 0.25
LN_EPS = 1e-5
LAMBDA_INIT = 0.8 - 0.6 * math.exp(-0.3 * 0)

LANES = 128
NEG_INF = float("-inf")
POS_INF = float("inf")
VMEM_LIMIT = 48 * 1024 * 1024

COL_BR = 0
COL_Q, COL_K, COL_V = 2048, 2560, 3072
COL_MLQ, COL_MLK, COL_MLV, COL_MLO = 3584, 4096, 4608, 5120
COL_MLG = 5632
PROJ_COLS = COL_MLG + LANES
N_GATE_COLS = 4 * ML_HEADS
IN_MAIN = 3584


def _params(*sem, flags=None):
    return pltpu.CompilerParams(dimension_semantics=sem, vmem_limit_bytes=VMEM_LIMIT, flags=flags)


def _mm_kernel(x_ref, w_ref, o_ref):
    o_ref[...] = jnp.dot(x_ref[...].astype(jnp.bfloat16), w_ref[...],
                         preferred_element_type=jnp.float32).astype(o_ref.dtype)


def _matmul(x, w, tm, tn, out_dtype=jnp.float32):
    m, k = x.shape
    n = w.shape[1]
    tm, tn = min(tm, m), min(tn, n)
    return pl.pallas_call(
        _mm_kernel, grid=(m // tm, n // tn),
        in_specs=[pl.BlockSpec((tm, k), lambda i, j: (i, 0)),
                  pl.BlockSpec((k, tn), lambda i, j: (0, j))],
        out_specs=pl.BlockSpec((tm, tn), lambda i, j: (i, j)),
        out_shape=jax.ShapeDtypeStruct((m, n), out_dtype),
        compiler_params=_params("parallel", "parallel"), name="dense_matmul")(x, w)


def _split_bf16(a):
    hi = a.astype(jnp.bfloat16)
    lo = (a - hi.astype(jnp.float32)).astype(jnp.bfloat16)
    return hi, lo


def _dot3(a_hi, a_lo, b_hi, b_lo, dims):
    f = lambda a, b: lax.dot_general(a, b, (dims, ((), ())), preferred_element_type=jnp.float32)
    return f(a_hi, b_hi) + f(a_hi, b_lo) + f(a_lo, b_hi)


def _mm3_kernel(x_ref, wh_ref, wl_ref, o_ref):
    xh, xl = _split_bf16(x_ref[...])
    o_ref[...] = _dot3(xh, xl, wh_ref[...], wl_ref[...], ((1,), (0,)))


def _matmul3(x, w_hi, w_lo, tm, tn):
    m, k = x.shape
    n = w_hi.shape[1]
    tm, tn = min(tm, m), min(tn, n)
    return pl.pallas_call(
        _mm3_kernel, grid=(m // tm, n // tn),
        in_specs=[pl.BlockSpec((tm, k), lambda i, j: (i, 0)),
                  pl.BlockSpec((k, tn), lambda i, j: (0, j)),
                  pl.BlockSpec((k, tn), lambda i, j: (0, j))],
        out_specs=pl.BlockSpec((tm, tn), lambda i, j: (i, j)),
        out_shape=jax.ShapeDtypeStruct((m, n), jnp.float32),
        compiler_params=_params("parallel", "parallel"), name="dense_matmul3")(x, w_hi, w_lo)


def _rope_kernel(q_ref, k_ref, v_ref, c_ref, s1_ref, s2_ref, o_ref):
    c, s1, s2 = c_ref[...], s1_ref[...], s2_ref[...]
    for which, src in enumerate((q_ref, k_ref)):
        for hd in range(DA_HEADS):
            x = src[:, hd * LANES:(hd + 1) * LANES]
            r = x * c + pltpu.roll(x, LANES - 8, 1) * s1 + pltpu.roll(x, 8, 1) * s2
            if which == 0:
                r = r * (DA_QK_DIM ** -0.5)
            col = which * DA_WIDTH + hd * LANES
            o_ref[:, col:col + LANES] = r.astype(jnp.bfloat16)
    o_ref[:, 2 * DA_WIDTH:3 * DA_WIDTH] = v_ref[...].astype(jnp.bfloat16)


def _rope_tables(seq):
    pos = jnp.arange(seq, dtype=jnp.float32)
    inv_freq = ROPE_THETA ** (-jnp.arange(0, ROPE_DIM, 2, dtype=jnp.float32) / ROPE_DIM)
    ang = pos[:, None] * inv_freq[None, :]
    cos, sin = jnp.cos(ang), jnp.sin(ang)
    half = ROPE_DIM // 2
    zeros = jnp.zeros((seq, DA_QK_DIM - ROPE_DIM), jnp.float32)
    z8 = jnp.zeros((seq, half), jnp.float32)
    c64 = jnp.concatenate([cos, cos, zeros + 1.0], -1)
    s1_64 = jnp.concatenate([-sin, z8, zeros], -1)
    s2_64 = jnp.concatenate([z8, sin, zeros], -1)
    tile2 = lambda t: jnp.concatenate([t, t], -1)
    return tile2(c64), tile2(s1_64), tile2(s2_64)


def _rope_cast(proj, seq, tm):
    t = proj.shape[0]
    tm = min(tm, seq)
    nsb = seq // tm
    c, s1, s2 = _rope_tables(seq)
    tab = pl.BlockSpec((tm, LANES), lambda i: (i % nsb, 0))
    src = lambda col: pl.BlockSpec((tm, DA_WIDTH), lambda i: (i, col // DA_WIDTH))
    return pl.pallas_call(
        _rope_kernel, grid=(t // tm,),
        in_specs=[src(COL_Q), src(COL_K), src(COL_V), tab, tab, tab],
        out_specs=pl.BlockSpec((tm, 3 * DA_WIDTH), lambda i: (i, 0)),
        out_shape=jax.ShapeDtypeStruct((t, 3 * DA_WIDTH), jnp.bfloat16),
        compiler_params=_params("parallel"), name="rope_cast")(proj, proj, proj, c, s1, s2)


def _attn_kernel(lam_ref, g_ref, q_ref, k_ref, v_ref, o_ref, m_ref, acc_ref, sa_ref, sb_ref, *, tk):
    nk = k_ref.shape[0] // tk
    m_ref[...] = jnp.full(m_ref.shape, NEG_INF, jnp.float32)
    acc_ref[...] = jnp.zeros(acc_ref.shape, jnp.float32)
    lane = lax.broadcasted_iota(jnp.int32, (tk, LANES), 1)
    ones_col = jnp.where(lane == 0, 1.0, 0.0).astype(jnp.bfloat16)

    def scores(j, s_ref):
        k = k_ref[pl.ds(pl.multiple_of(j * tk, tk), tk), :]
        for mp in range(2):
            q = q_ref[:, mp * DA_QK_DIM:(mp + 1) * DA_QK_DIM]
            s_ref[mp] = lax.dot_general(q, k[:, mp * DA_QK_DIM:(mp + 1) * DA_QK_DIM],
                                        (((1,), (1,)), ((), ())), preferred_element_type=jnp.float32)

    def accumulate(j, s_ref):
        vext = jnp.concatenate([v_ref[pl.ds(pl.multiple_of(j * tk, tk), tk), :], ones_col], axis=1)
        for mp in range(2):
            s = s_ref[mp]
            m_old = m_ref[mp]
            m_new = jnp.maximum(m_old, jnp.max(s, axis=-1, keepdims=True))
            p = jnp.exp(s - m_new).astype(jnp.bfloat16)
            pv = jnp.dot(p, vext, preferred_element_type=jnp.float32)
            acc_ref[mp] = jnp.exp(m_old - m_new) * acc_ref[mp] + pv
            m_ref[mp] = m_new

    scores(0, sa_ref)

    def body(jj, carry):
        j = 2 * jj
        scores(j + 1, sb_ref)
        accumulate(j, sa_ref)
        scores(j + 2, sa_ref)
        accumulate(j + 1, sb_ref)
        return carry

    lax.fori_loop(0, nk // 2 - 1, body, 0)
    scores(nk - 1, sb_ref)
    accumulate(nk - 2, sa_ref)
    accumulate(nk - 1, sb_ref)

    lp = lam_ref[...]
    lam = (jnp.exp(jnp.sum(lp[0:1] * lp[1:2], axis=-1, keepdims=True))
           - jnp.exp(jnp.sum(lp[2:3] * lp[3:4], axis=-1, keepdims=True)) + LAMBDA_INIT)
    a0, a1 = acc_ref[0], acc_ref[1]
    o = (a0[:, :DA_V_DIM] / a0[:, DA_V_DIM:DA_V_DIM + 1]
         - lam * (a1[:, :DA_V_DIM] / a1[:, DA_V_DIM:DA_V_DIM + 1]))
    y = o * lax.rsqrt(jnp.mean(o * o, axis=-1, keepdims=True) + LN_EPS)
    o_ref[...] = y * g_ref[...] * (1.0 - LAMBDA_INIT)


def _diff_attention(qkv, da_lambda, subln_g, batch, seq, tq, tk):
    t = qkv.shape[0]
    tq, tk = min(tq, seq), min(tk, seq // 2)
    assert (seq // tk) % 2 == 0, "the key loop is software-pipelined over chunk pairs"
    nq = seq // tq
    kern = functools.partial(_attn_kernel, tk=tk)
    return pl.pallas_call(
        kern, grid=(batch, DA_HEADS, nq),
        in_specs=[pl.BlockSpec((4, DA_QK_DIM), lambda b, h, i: (0, 0)),
                  pl.BlockSpec((1, DA_V_DIM), lambda b, h, i: (0, 0)),
                  pl.BlockSpec((tq, LANES), lambda b, h, i: (b * nq + i, h)),
                  pl.BlockSpec((seq, LANES), lambda b, h, i: (b, DA_HEADS + h)),
                  pl.BlockSpec((seq, LANES), lambda b, h, i: (b, 2 * DA_HEADS + h))],
        out_specs=pl.BlockSpec((tq, LANES), lambda b, h, i: (b * nq + i, h)),
        out_shape=jax.ShapeDtypeStruct((t, DA_WIDTH), jnp.float32),
        scratch_shapes=[pltpu.VMEM((2, tq, 1), jnp.float32),
                        pltpu.VMEM((2, tq, 2 * LANES), jnp.float32),
                        pltpu.VMEM((2, tq, tk), jnp.float32),
                        pltpu.VMEM((2, tq, tk), jnp.float32)],
        compiler_params=_params("parallel", "parallel", "parallel"),
        name="diff_attention")(da_lambda, subln_g.reshape(1, DA_V_DIM), qkv, qkv, qkv)


def _conv_kernel(prev_ref, cur_ref, next_ref, w_ref, b_ref, o_ref, *, nsb):
    i, c = pl.program_id(0), pl.program_id(1)
    tm = cur_ref.shape[0]
    sb = i % nsb
    prev = jnp.where(sb == 0, 0.0, prev_ref[...])
    nxt = jnp.where(sb == nsb - 1, 0.0, next_ref[...])
    ext = jnp.concatenate([prev, cur_ref[...], nxt], axis=0)
    n = tm + 16
    pad = (ML_CONV - 1) // 2
    acc = jnp.zeros((tm, cur_ref.shape[1]), jnp.float32) + b_ref[...]
    for w in range(ML_CONV):
        shift = (pad - w) % n
        z = ext if shift == 0 else pltpu.roll(ext, shift, 0)
        acc = acc + z[8:8 + tm] * w_ref[w:w + 1, :]
    y = acc * jax.nn.sigmoid(acc)
    y = y * jnp.where(c == 1, ML_HEAD_DIM ** -0.5, 1.0)
    o_ref[...] = y.astype(jnp.bfloat16)


def _conv_silu(proj, conv_w, conv_b, seq, tm):
    t = proj.shape[0]
    tm = min(tm, seq)
    nsb = seq // tm
    r8 = tm // 8
    nrow8 = t // 8
    cb0 = COL_MLQ // ML_WIDTH
    kern = functools.partial(_conv_kernel, nsb=nsb)
    return pl.pallas_call(
        kern, grid=(t // tm, 2),
        in_specs=[pl.BlockSpec((8, ML_WIDTH), lambda i, c: (jnp.maximum(i * r8 - 1, 0), cb0 + c)),
                  pl.BlockSpec((tm, ML_WIDTH), lambda i, c: (i, cb0 + c)),
                  pl.BlockSpec((8, ML_WIDTH), lambda i, c: (jnp.minimum((i + 1) * r8, nrow8 - 1), cb0 + c)),
                  pl.BlockSpec((ML_CONV, ML_WIDTH), lambda i, c: (0, c)),
                  pl.BlockSpec((1, ML_WIDTH), lambda i, c: (0, c))],
        out_specs=pl.BlockSpec((tm, ML_WIDTH), lambda i, c: (i, c)),
        out_shape=jax.ShapeDtypeStruct((t, 2 * ML_WIDTH), jnp.bfloat16),
        compiler_params=_params("parallel", "parallel"),
        name="conv_silu")(proj, proj, proj, conv_w, conv_b.reshape(1, 2 * ML_WIDTH))


def _seg_scan(x, pos, seq, op, fill, reverse):
    y = x
    d = 1
    while d < ML_CHUNK:
        if reverse:
            y = op(y, jnp.where(pos < ML_CHUNK - d, pltpu.roll(y, seq - d, 1), fill))
        else:
            y = op(y, jnp.where(pos >= d, pltpu.roll(y, d, 1), fill))
        d *= 2
    return y


def _gate_dir(i_pre, logf, pos, lane, seq, reverse):
    bj = _seg_scan(logf, pos, seq, jnp.add, 0.0, reverse)
    last = (pos == 0) if reverse else (pos == ML_CHUNK - 1)
    b_tot = _seg_scan(jnp.where(last, bj, NEG_INF), pos, seq, jnp.maximum, NEG_INF, not reverse)
    w = i_pre - bj
    cmax_w = _seg_scan(w, pos, seq, jnp.maximum, NEG_INF, reverse)
    g = b_tot - bj + i_pre
    g_max = jnp.maximum(_seg_scan(g, pos, seq, jnp.maximum, NEG_INF, False),
                        _seg_scan(g, pos, seq, jnp.maximum, NEG_INF, True))
    acc_a, acc_g = b_tot, g_max
    d = ML_CHUNK
    while d < seq:
        if reverse:
            ok = lane < seq - d
            pa = jnp.where(ok, pltpu.roll(acc_a, seq - d, 1), 0.0)
            pg = jnp.where(ok, pltpu.roll(acc_g, seq - d, 1), NEG_INF)
        else:
            ok = lane >= d
            pa = jnp.where(ok, pltpu.roll(acc_a, d, 1), 0.0)
            pg = jnp.where(ok, pltpu.roll(acc_g, d, 1), NEG_INF)
        acc_g = jnp.maximum(pg + acc_a, acc_g)
        acc_a = pa + acc_a
        d *= 2
    m_after = jnp.maximum(acc_a + STAB_INIT, acc_g)
    if reverse:
        m_before = jnp.where(lane < seq - ML_CHUNK, pltpu.roll(m_after, seq - ML_CHUNK, 1), STAB_INIT)
    else:
        m_before = jnp.where(lane >= ML_CHUNK, pltpu.roll(m_after, ML_CHUNK, 1), STAB_INIT)
    mr_rel = jnp.maximum(m_before, cmax_w)
    m_row = bj + mr_rel
    u = -mr_rel
    w_inter = jnp.exp(m_before - mr_rel)
    e_negm = jnp.exp(-m_row)
    wk = jnp.exp(g - m_after)
    a = jnp.exp(b_tot + m_before - m_after)
    return u, w, w_inter, e_negm, wk, a


def _gate_kernel(g_ref, b_ref, o_ref):
    seq = g_ref.shape[2]
    g = g_ref[0] + b_ref[...]
    i_pre, f_pre = g[0:8], g[8:16]
    logf = jnp.minimum(f_pre, 0.0) - jnp.log1p(jnp.exp(-jnp.abs(f_pre)))
    lane = lax.broadcasted_iota(jnp.int32, (2 * ML_HEADS, seq), 1)
    row = lax.broadcasted_iota(jnp.int32, (2 * ML_HEADS, seq), 0)
    pos = lane % ML_CHUNK
    fwd = _gate_dir(i_pre, logf, pos, lane, seq, False)
    bwd = _gate_dir(i_pre, logf, pos, lane, seq, True)
    for q, (af, ab) in enumerate(zip(fwd, bwd)):
        o_ref[0, q] = jnp.where(row < ML_HEADS, af, ab)


def _gate_scans(gates_t, gate_b):
    b, _, seq = gates_t.shape
    return pl.pallas_call(
        _gate_kernel, grid=(b,),
        in_specs=[pl.BlockSpec((1, 16, seq), lambda i: (i, 0, 0)),
                  pl.BlockSpec((16, 1), lambda i: (0, 0))],
        out_specs=pl.BlockSpec((1, 6, 8, seq), lambda i: (i, 0, 0, 0)),
        out_shape=jax.ShapeDtypeStruct((b, 6, 8, seq), jnp.float32),
        compiler_params=_params("parallel"), name="mlstm_gate_scans")(gates_t, gate_b.reshape(16, 1))


Q_U, Q_WINTER, Q_ENEGM, Q_WK, Q_A = 0, 1, 2, 3, 4


def _mlstm_kernel(qf_ref, kf_ref, vf_ref, cf_ref, wf_ref, qb_ref, kb_ref, vb_ref, cb_ref, wb_ref,
                  of_ref, ob_ref, ce_ref, *, chunks):
    @pl.when(pl.program_id(1) == 0)
    def _():
        ce_ref[...] = jnp.zeros(ce_ref.shape, jnp.float32)

    L = ML_CHUNK
    row = lax.broadcasted_iota(jnp.int32, (L, L), 0)
    col = lax.broadcasted_iota(jnp.int32, (L, L), 1)
    lane = lax.broadcasted_iota(jnp.int32, (L, LANES), 1)
    ones_col = jnp.where(lane == 0, 1.0, 0.0).astype(jnp.bfloat16)
    dirs = ((0, qf_ref, kf_ref, vf_ref, cf_ref, wf_ref, of_ref, col <= row, range(chunks)),
            (1, qb_ref, kb_ref, vb_ref, cb_ref, wb_ref, ob_ref, col >= row, range(chunks - 1, -1, -1)))
    for d, q_ref, k_ref, v_ref, c_ref, w_ref, o_ref, mask, order in dirs:
        for c in order:
            r0 = c * L
            for h in range(ML_HEADS):
                hs = slice(h * ML_HEAD_DIM, (h + 1) * ML_HEAD_DIM)
                cidx = lambda qn: c_ref[r0:r0 + L, qn * 8 + d * 4 + h:qn * 8 + d * 4 + h + 1]
                q = q_ref[r0:r0 + L, hs]
                k = k_ref[r0:r0 + L, hs]
                vext = jnp.concatenate([v_ref[r0:r0 + L, hs].astype(jnp.bfloat16), ones_col], axis=1)
                s = lax.dot_general(q, k, (((1,), (1,)), ((), ())), preferred_element_type=jnp.float32)
                w_row = w_ref[0, d * 4 + h:d * 4 + h + 1, r0:r0 + L]
                dec = jnp.where(mask, jnp.exp(cidx(Q_U) + w_row), 0.0)
                sd = (s * dec).astype(jnp.bfloat16)
                ce = ce_ref[d * 4 + h]
                inter = jnp.dot(q, ce.astype(jnp.bfloat16), preferred_element_type=jnp.float32)
                intra = jnp.dot(sd, vext, preferred_element_type=jnp.float32)
                tot = cidx(Q_WINTER) * inter + intra
                den = jnp.maximum(jnp.abs(tot[:, ML_HEAD_DIM:ML_HEAD_DIM + 1]), cidx(Q_ENEGM))
                o_ref[r0:r0 + L, hs] = tot[:, :ML_HEAD_DIM] / den
                kw_t = (k.astype(jnp.float32) * cidx(Q_WK)).T.astype(jnp.bfloat16)
                upd = jnp.dot(kw_t, vext, preferred_element_type=jnp.float32)
                a = c_ref[r0:r0 + 1, Q_A * 8 + d * 4 + h:Q_A * 8 + d * 4 + h + 1]
                ce_ref[d * 4 + h] = a * ce + upd


def _mlstm_scan(qk_conv, proj, col_tab, w_tab, batch, seq, chunks):
    t = qk_conv.shape[0]
    rows = chunks * ML_CHUNK
    ns = seq // rows
    vcb = COL_MLV // ML_WIDTH
    ncol = col_tab.shape[1]
    fw = lambda b, j: b * ns + j
    bw = lambda b, j: b * ns + ns - 1 - j
    def specs(rb, lb):
        return [pl.BlockSpec((rows, ML_WIDTH), lambda b, j: (rb(b, j), 0)),
                pl.BlockSpec((rows, ML_WIDTH), lambda b, j: (rb(b, j), 1)),
                pl.BlockSpec((rows, ML_WIDTH), lambda b, j: (rb(b, j), vcb)),
                pl.BlockSpec((rows, ncol), lambda b, j: (rb(b, j), 0)),
                pl.BlockSpec((1, 8, rows), lambda b, j: (b, 0, lb(j)))]
    out_f = pl.BlockSpec((rows, ML_WIDTH), lambda b, j: (fw(b, j), 0))
    out_b = pl.BlockSpec((rows, ML_WIDTH), lambda b, j: (bw(b, j), 0))
    kern = functools.partial(_mlstm_kernel, chunks=chunks)
    return pl.pallas_call(
        kern, grid=(batch, ns),
        in_specs=specs(fw, lambda j: j) + specs(bw, lambda j: ns - 1 - j),
        out_specs=[out_f, out_b],
        out_shape=[jax.ShapeDtypeStruct((t, ML_WIDTH), jnp.float32)] * 2,
        scratch_shapes=[pltpu.VMEM((2 * ML_HEADS, ML_HEAD_DIM, 2 * LANES), jnp.float32)],
        compiler_params=_params("parallel", "arbitrary"),
        name="mlstm_scan")(qk_conv, qk_conv, proj, col_tab, w_tab,
                           qk_conv, qk_conv, proj, col_tab, w_tab)


def _layer_norm_rows(z, g, b):
    mu = jnp.mean(z, axis=-1, keepdims=True)
    zc = z - mu
    var = jnp.mean(zc * zc, axis=-1, keepdims=True)
    return zc * lax.rsqrt(var + LN_EPS) * g + b


def _merge_kernel(x_ref, ya_ref, hf_ref, hb_ref, mo_ref, g0_ref, g1_ref, ng_ref,
                  wa_ref, wm_ref, wo_ref, lg_ref, lb_ref, o_ref, ot_ref, otb_ref):
    h = hf_ref[...] + hb_ref[...]
    parts = []
    for hd in range(ML_HEADS):
        hh = h[:, hd * ML_HEAD_DIM:(hd + 1) * ML_HEAD_DIM]
        mu = jnp.mean(hh, axis=-1, keepdims=True)
        hc = hh - mu
        var = jnp.mean(hc * hc, axis=-1, keepdims=True)
        parts.append(hc * lax.rsqrt(var + LN_EPS))
    ym = jnp.concatenate(parts, axis=1) * ng_ref[...] * jax.nn.sigmoid(mo_ref[...])
    pa = jnp.dot(ya_ref[...].astype(jnp.bfloat16), wa_ref[...], preferred_element_type=jnp.float32)
    pm = jnp.dot(ym.astype(jnp.bfloat16), wm_ref[...], preferred_element_type=jnp.float32)
    merged = jax.nn.sigmoid(g0_ref[...]) * pa + jax.nn.sigmoid(g1_ref[...]) * pm
    mix = jnp.dot(merged.astype(jnp.bfloat16), wo_ref[...], preferred_element_type=jnp.float32)
    x1 = _layer_norm_rows(DN_ALPHA * x_ref[...] + mix, lg_ref[...], lb_ref[...])
    o_ref[...] = x1
    x1t = x1.T
    ot_ref[...] = x1t
    otb_ref[...] = x1t.astype(jnp.bfloat16)


def _merge(xt, y_attn, h_fw, h_bw, proj, norm_g, wa, wm, wo, ln_g, ln_b, tm):
    t = xt.shape[0]
    tm = min(tm, t)
    row = lambda w, cb: pl.BlockSpec((tm, w), lambda i: (i, cb))
    full = lambda a: pl.BlockSpec(a.shape, lambda i: (0, 0))
    vec = lambda a: a.reshape(1, -1)
    args = (xt, y_attn, h_fw, h_bw, proj, proj, proj, vec(norm_g), wa, wm, wo, vec(ln_g), vec(ln_b))
    in_specs = [row(D_MODEL, 0), row(DA_WIDTH, 0), row(ML_WIDTH, 0), row(ML_WIDTH, 0),
                row(ML_WIDTH, COL_MLO // ML_WIDTH),
                row(D_MODEL, COL_BR // D_MODEL), row(D_MODEL, COL_BR // D_MODEL + 1)]
    in_specs += [full(a) for a in args[7:]]
    return pl.pallas_call(
        _merge_kernel, grid=(t // tm,), in_specs=in_specs,
        out_specs=[pl.BlockSpec((tm, D_MODEL), lambda i: (i, 0)),
                   pl.BlockSpec((D_MODEL, tm), lambda i: (0, i)),
                   pl.BlockSpec((D_MODEL, tm), lambda i: (0, i))],
        out_shape=[jax.ShapeDtypeStruct((t, D_MODEL), jnp.float32),
                   jax.ShapeDtypeStruct((D_MODEL, t), jnp.float32),
                   jax.ShapeDtypeStruct((D_MODEL, t), jnp.bfloat16)],
        compiler_params=_params("parallel"), name="merge_ln")(*args)


NOT_TOP = float(PEER_TOPK)
N_CAND_ROWS = 80


def _topk_rows(s, dst_ref):
    rank = jnp.full(s.shape, NOT_TOP, jnp.float32)
    for x in range(PEER_TOPK):
        m = jnp.max(s, axis=0, keepdims=True)
        dst_ref[x:x + 1, :] = m
        hit = s == m
        rank = jnp.where(hit, float(x), rank)
        s = jnp.where(hit, NEG_INF, s)
    return rank


def _peer_prep_kernel(q_ref, kh_ref, kl_ref, ry_ref, cnt_ref, amp_ref, eb_ref, a_ref, b_ref, sum_ref):
    half = N_KEYS
    sc = []
    for p in range(2):
        qh, ql = _split_bf16(q_ref[:, p * half:(p + 1) * half])
        sc.append(_dot3(kh_ref[p], kl_ref[p], qh, ql, ((1,), (1,))))
    s1, s2 = sc
    rank1 = _topk_rows(s1, a_ref)
    rank2 = _topk_rows(s2, b_ref)
    a, b = a_ref[...], b_ref[...]
    h8 = PEER_TOPK // 2
    sum_ref[0:PEER_TOPK, :] = a + b[0:1, :]
    for y in range(1, h8):
        sum_ref[PEER_TOPK + (y - 1) * h8:PEER_TOPK + y * h8, :] = a[0:h8, :] + b[y:y + 1, :]
    sum_ref[N_CAND_ROWS - h8:N_CAND_ROWS, :] = b[h8:PEER_TOPK, :] + a[0:1, :]
    sums = sum_ref[...]
    rest = sums
    for _ in range(PEER_TOPK):
        tau = jnp.max(rest, axis=0, keepdims=True)
        rest = jnp.where(rest == tau, NEG_INF, rest)
    top = a[0:1, :] + b[0:1, :]
    z = jnp.sum(jnp.where(sums >= tau, jnp.exp(sums - top), 0.0), axis=0, keepdims=True)
    cntx = jnp.zeros(a.shape, jnp.float32)
    for y in range(PEER_TOPK):
        cntx = cntx + jnp.where(a + b[y:y + 1, :] >= tau, 1.0, 0.0)
    cnt = jnp.zeros(s1.shape, jnp.float32)
    for x in range(PEER_TOPK):
        cnt = jnp.where(rank1 == float(x), cntx[x:x + 1, :], cnt)
    results = ((ry_ref, rank2.astype(jnp.bfloat16)),
               (cnt_ref, cnt),
               (amp_ref, jnp.exp(s1 - a[0:1, :])),
               (eb_ref, (jnp.exp(s2 - b[0:1, :]) / z).astype(jnp.bfloat16)))
    for ref, val in results:
        for tc in range(ref.shape[1]):
            ref[0, tc] = val[:, tc * LANES:(tc + 1) * LANES]


def _peer_prep(q, keys_hi, keys_lo, tt):
    t = q.shape[0]
    tt = min(tt, t)
    qdim = 2 * N_KEYS
    shape = (PEER_HEADS, t // LANES, N_KEYS, LANES)
    outs = [jax.ShapeDtypeStruct(shape, dt) for dt in (jnp.bfloat16, jnp.float32, jnp.float32, jnp.bfloat16)]
    ospec = pl.BlockSpec((1, tt // LANES, N_KEYS, LANES), lambda i, h: (h, i, 0, 0))
    kspec = pl.BlockSpec((2, N_KEYS, N_KEYS), lambda i, h: (0, 0, 0))
    return pl.pallas_call(
        _peer_prep_kernel, grid=(t // tt, PEER_HEADS),
        in_specs=[pl.BlockSpec((tt, qdim), lambda i, h: (i, h)), kspec, kspec],
        out_specs=[ospec] * 4, out_shape=outs,
        scratch_shapes=[pltpu.VMEM((PEER_TOPK, tt), jnp.float32),
                        pltpu.VMEM((PEER_TOPK, tt), jnp.float32),
                        pltpu.VMEM((N_CAND_ROWS, tt), jnp.float32)],
        compiler_params=_params("parallel", "parallel"), name="peer_prep")(q, keys_hi, keys_lo)


PEER_PIPE_LAG = 1
PEER_CHUNKS = 1


def _peer_main_kernel(xb_ref, xf_ref, u_ref, vt_ref, ry_ref, cnt_ref, amp_ref, eb_ref, lg_ref, lb_ref,
                      o_ref, acc_ref, pre0_ref, pre1_ref, *, rows_per_step, n_blocks):
    e = pl.program_id(1)
    tb = xb_ref.shape[1]
    bf = jnp.bfloat16

    @pl.when(e == 0)
    def _():
        acc_ref[...] = jnp.zeros(acc_ref.shape, jnp.float32)
        pre0_ref[...] = jnp.zeros(pre0_ref.shape, jnp.float32)
        pre1_ref[...] = jnp.zeros(pre1_ref.shape, jnp.float32)

    blk = jnp.clip(e - 1, 0, n_blocks - 1)

    def weigh(ii, pre_r):
        i = blk * rows_per_step + ii
        rows = slice(ii * N_KEYS, (ii + 1) * N_KEYS)
        zero = jnp.zeros((N_KEYS, LANES), bf)
        tiles = []
        for tc in range(tb // LANES):
            cols = slice(tc * LANES, (tc + 1) * LANES)
            spread = lambda ref, h: jnp.broadcast_to(
                ref[h, tc, pl.ds(i, 1), :].astype(bf), (N_KEYS, LANES))
            g = zero
            for h in range(PEER_HEADS):
                sel = jnp.where(ry_ref[h, tc] < spread(cnt_ref, h), eb_ref[h, tc], zero)
                g = g + spread(amp_ref, h) * sel
            pre = pre_r[rows, cols]
            act = 0.5 * pre * (1.0 + lax.erf(pre * (2.0 ** -0.5)))
            tiles.append(g * act.astype(bf))
        return jnp.concatenate(tiles, axis=1)

    def step(pre_w, pre_r):
        e_rows = u_ref.shape[0] // PEER_CHUNKS
        per = rows_per_step // PEER_CHUNKS
        chunk = lambda c: slice(c * e_rows, (c + 1) * e_rows)

        def first_matmul(c):
            pre_w[chunk(c), :] = jnp.dot(u_ref[chunk(c), :], xb_ref[...], preferred_element_type=jnp.float32)

        first_matmul(0)
        for c in range(PEER_CHUNKS):
            if c + 1 < PEER_CHUNKS:
                first_matmul(c + 1)
            wgt = jnp.concatenate([weigh(ii, pre_r) for ii in range(c * per, (c + 1) * per)], axis=0)
            acc_ref[...] += jnp.dot(vt_ref[:, chunk(c)], wgt, preferred_element_type=jnp.float32)

    for par in range(2):
        @pl.when(e % 2 == par)
        def _():
            pre = (pre0_ref, pre1_ref)
            step(pre[par], pre[1 - par])

    @pl.when(e == n_blocks + PEER_PIPE_LAG - 1)
    def _():
        z = DN_ALPHA * xf_ref[...] + acc_ref[...]
        mu = jnp.mean(z, axis=0, keepdims=True)
        zc = z - mu
        var = jnp.mean(zc * zc, axis=0, keepdims=True)
        o_ref[...] = (zc * lax.rsqrt(var + LN_EPS)).T * lg_ref[...] + lb_ref[...]


def _peer_main(x1t_bf, x1t, u_bf, vt_bf, ry, cnt, amp, eb, ln_g, ln_b, tb, eb_rows):
    d, t = x1t.shape
    tb = min(tb, t)
    rows_per_step = eb_rows // N_KEYS
    n_blocks = N_EXPERTS // eb_rows
    tok = lambda: pl.BlockSpec((d, tb), lambda i, e: (0, i))
    tab = lambda: pl.BlockSpec((PEER_HEADS, tb // LANES, N_KEYS, LANES), lambda i, e: (0, i, 0, 0))
    colv = lambda: pl.BlockSpec((1, d), lambda i, e: (0, 0))
    kern = functools.partial(_peer_main_kernel, rows_per_step=rows_per_step, n_blocks=n_blocks)
    return pl.pallas_call(
        kern, grid=(t // tb, n_blocks + PEER_PIPE_LAG),
        in_specs=[tok(), tok(),
                  pl.BlockSpec((eb_rows, d), lambda i, e: (jnp.minimum(e, n_blocks - 1), 0)),
                  pl.BlockSpec((d, eb_rows), lambda i, e: (0, jnp.clip(e - PEER_PIPE_LAG, 0, n_blocks - 1))),
                  tab(), tab(), tab(), tab(), colv(), colv()],
        out_specs=pl.BlockSpec((tb, d), lambda i, e: (i, 0)),
        out_shape=jax.ShapeDtypeStruct((t, d), jnp.float32),
        scratch_shapes=[pltpu.VMEM((d, tb), jnp.float32),
                        pltpu.VMEM((eb_rows, tb), jnp.float32),
                        pltpu.VMEM((eb_rows, tb), jnp.float32)],
        compiler_params=_params("parallel", "arbitrary"),
        name="peer_dense")(x1t_bf, x1t, u_bf, vt_bf, ry, cnt, amp, eb,
                           ln_g.reshape(1, d), ln_b.reshape(1, d))


def _token_mixing(xt, batch, seq, w_in, da_lambda, da_subln_g, ml_conv_w, ml_conv_b, ml_gate_b,
                  ml_norm_g, w_branch_attn, w_branch_mlstm, w_out, ln1_g, ln1_b):
    t = batch * seq
    bf = jnp.bfloat16
    zpad = jnp.zeros((D_MODEL, LANES - N_GATE_COLS), jnp.float32)
    w_cols = jnp.concatenate([w_in[:, IN_MAIN + N_GATE_COLS:], w_in[:, :IN_MAIN],
                              w_in[:, IN_MAIN:IN_MAIN + N_GATE_COLS], zpad], axis=1).astype(bf)
    proj = _matmul(xt, w_cols, 512, 1152)

    qkv = _rope_cast(proj, seq, 512)
    y_attn = _diff_attention(qkv, da_lambda, da_subln_g, batch, seq, 512, 1024)

    qk_conv = _conv_silu(proj, ml_conv_w, ml_conv_b, seq, 512)
    order = jnp.array([0, 2, 1, 3])
    gates_t = proj[:, COL_MLG:COL_MLG + N_GATE_COLS].reshape(batch, seq, 4, ML_HEADS)[:, :, order]
    gates_t = gates_t.reshape(batch, seq, N_GATE_COLS).transpose(0, 2, 1)
    gtab = _gate_scans(gates_t, ml_gate_b[order].reshape(N_GATE_COLS))
    col_tab = jnp.concatenate([gtab[:, 0], gtab[:, 2], gtab[:, 3], gtab[:, 4], gtab[:, 5]], axis=1)
    col_tab = col_tab.transpose(0, 2, 1).reshape(t, 5 * 8)
    w_tab = gtab[:, 1]
    h_fw, h_bw = _mlstm_scan(qk_conv, proj, col_tab, w_tab, batch, seq,
                             min(4, seq // ML_CHUNK))

    return _merge(xt, y_attn, h_fw, h_bw, proj, ml_norm_g,
                  w_branch_attn.astype(bf), w_branch_mlstm.astype(bf), w_out.astype(bf),
                  ln1_g, ln1_b, 256)


def _peer_layer(x1, x1t, x1t_bf, peer_w_q, peer_sub_keys, peer_u, peer_v, ln2_g, ln2_b):
    bf = jnp.bfloat16
    wq_hi, wq_lo = _split_bf16(peer_w_q)
    q = _matmul3(x1, wq_hi, wq_lo, 512, 512)
    k_hi, k_lo = _split_bf16(peer_sub_keys)
    ry, cnt, amp, eb = _peer_prep(q, k_hi, k_lo, 256)
    return _peer_main(x1t_bf, x1t, peer_u.astype(bf), peer_v.T.astype(bf),
                      ry, cnt, amp, eb, ln2_g, ln2_b, 512, 1024)


def kernel(x, w_in, da_lambda, da_subln_g, ml_conv_w, ml_conv_b, ml_gate_b, ml_norm_g,
           w_branch_attn, w_branch_mlstm, w_out, ln1_g, ln1_b,
           peer_w_q, peer_sub_keys, peer_u, peer_v, ln2_g, ln2_b):
    batch, seq, d = x.shape
    xt = x.reshape(batch * seq, d)
    x1, x1t, x1t_bf = _token_mixing(xt, batch, seq, w_in[0], da_lambda[0], da_subln_g[0], ml_conv_w[0],
                                    ml_conv_b[0], ml_gate_b[0], ml_norm_g[0], w_branch_attn[0],
                                    w_branch_mlstm[0], w_out[0], ln1_g[0], ln1_b[0])
    out = _peer_layer(x1, x1t, x1t_bf, peer_w_q[0], peer_sub_keys[0], peer_u[0], peer_v[0],
                      ln2_g[0], ln2_b[0])
    return out.reshape(batch, seq, d)
```

```python
import functools
import math

import jax
import jax.numpy as jnp
from jax import lax
from jax.experimental import pallas as pl
from jax.experimental.pallas import tpu as pltpu

D_MODEL = 1024
DA_QK_DIM = 64
DA_V_DIM = 128
DA_HEADS = 4
DA_WIDTH = 512
ROPE_DIM = 16
ROPE_THETA = 500000.0
ML_WIDTH = 512
ML_HEADS = 4
ML_HEAD_DIM = 128
ML_CONV = 5
ML_CHUNK = 64
STAB_INIT = -1e30
N_KEYS = 128
N_EXPERTS = N_KEYS * N_KEYS
PEER_HEADS = 8
PEER_TOPK = 16
DEPTH = 1
DN_ALPHA = (2 * DEPTH) ** 0.25
LN_EPS = 1e-5
LAMBDA_INIT = 0.8 - 0.6 * math.exp(-0.3 * 0)

LANES = 128
NEG_INF = float("-inf")
POS_INF = float("inf")
VMEM_LIMIT = 48 * 1024 * 1024

COL_BR = 0
COL_Q, COL_K, COL_V = 2048, 2560, 3072
COL_MLQ, COL_MLK, COL_MLV, COL_MLO = 3584, 4096, 4608, 5120
COL_MLG = 5632
PROJ_COLS = COL_MLG + LANES
N_GATE_COLS = 4 * ML_HEADS
IN_MAIN = 3584


def _params(*sem, flags=None):
    return pltpu.CompilerParams(dimension_semantics=sem, vmem_limit_bytes=VMEM_LIMIT, flags=flags)


def _mm_kernel(x_ref, w_ref, o_ref):
    o_ref[...] = jnp.dot(x_ref[...].astype(jnp.bfloat16), w_ref[...],
                         preferred_element_type=jnp.float32).astype(o_ref.dtype)


def _matmul(x, w, tm, tn, out_dtype=jnp.float32):
    m, k = x.shape
    n = w.shape[1]
    tm, tn = min(tm, m), min(tn, n)
    return pl.pallas_call(
        _mm_kernel, grid=(m // tm, n // tn),
        in_specs=[pl.BlockSpec((tm, k), lambda i, j: (i, 0)),
                  pl.BlockSpec((k, tn), lambda i, j: (0, j))],
        out_specs=pl.BlockSpec((tm, tn), lambda i, j: (i, j)),
        out_shape=jax.ShapeDtypeStruct((m, n), out_dtype),
        compiler_params=_params("parallel", "parallel"), name="dense_matmul")(x, w)


def _split_bf16(a):
    hi = a.astype(jnp.bfloat16)
    lo = (a - hi.astype(jnp.float32)).astype(jnp.bfloat16)
    return hi, lo


def _dot3(a_hi, a_lo, b_hi, b_lo, dims):
    f = lambda a, b: lax.dot_general(a, b, (dims, ((), ())), preferred_element_type=jnp.float32)
    return f(a_hi, b_hi) + f(a_hi, b_lo) + f(a_lo, b_hi)


def _mm3_kernel(x_ref, wh_ref, wl_ref, o_ref):
    xh, xl = _split_bf16(x_ref[...])
    o_ref[...] = _dot3(xh, xl, wh_ref[...], wl_ref[...], ((1,), (0,)))


def _matmul3(x, w_hi, w_lo, tm, tn):
    m, k = x.shape
    n = w_hi.shape[1]
    tm, tn = min(tm, m), min(tn, n)
    return pl.pallas_call(
        _mm3_kernel, grid=(m // tm, n // tn),
        in_specs=[pl.BlockSpec((tm, k), lambda i, j: (i, 0)),
                  pl.BlockSpec((k, tn), lambda i, j: (0, j)),
                  pl.BlockSpec((k, tn), lambda i, j: (0, j))],
        out_specs=pl.BlockSpec((tm, tn), lambda i, j: (i, j)),
        out_shape=jax.ShapeDtypeStruct((m, n), jnp.float32),
        compiler_params=_params("parallel", "parallel"), name="dense_matmul3")(x, w_hi, w_lo)


def _rope_kernel(q_ref, k_ref, v_ref, c_ref, s1_ref, s2_ref, o_ref):
    c, s1, s2 = c_ref[...], s1_ref[...], s2_ref[...]
    for which, src in enumerate((q_ref, k_ref)):
        for hd in range(DA_HEADS):
            x = src[:, hd * LANES:(hd + 1) * LANES]
            r = x * c + pltpu.roll(x, LANES - 8, 1) * s1 + pltpu.roll(x, 8, 1) * s2
            if which == 0:
                r = r * (DA_QK_DIM ** -0.5)
            col = which * DA_WIDTH + hd * LANES
            o_ref[:, col:col + LANES] = r.astype(jnp.bfloat16)
    o_ref[:, 2 * DA_WIDTH:3 * DA_WIDTH] = v_ref[...].astype(jnp.bfloat16)


def _rope_tables(seq):
    pos = jnp.arange(seq, dtype=jnp.float32)
    inv_freq = ROPE_THETA ** (-jnp.arange(0, ROPE_DIM, 2, dtype=jnp.float32) / ROPE_DIM)
    ang = pos[:, None] * inv_freq[None, :]
    cos, sin = jnp.cos(ang), jnp.sin(ang)
    half = ROPE_DIM // 2
    zeros = jnp.zeros((seq, DA_QK_DIM - ROPE_DIM), jnp.float32)
    z8 = jnp.zeros((seq, half), jnp.float32)
    c64 = jnp.concatenate([cos, cos, zeros + 1.0], -1)
    s1_64 = jnp.concatenate([-sin, z8, zeros], -1)
    s2_64 = jnp.concatenate([z8, sin, zeros], -1)
    tile2 = lambda t: jnp.concatenate([t, t], -1)
    return tile2(c64), tile2(s1_64), tile2(s2_64)


def _rope_cast(proj, seq, tm):
    t = proj.shape[0]
    tm = min(tm, seq)
    nsb = seq // tm
    c, s1, s2 = _rope_tables(seq)
    tab = pl.BlockSpec((tm, LANES), lambda i: (i % nsb, 0))
    src = lambda col: pl.BlockSpec((tm, DA_WIDTH), lambda i: (i, col // DA_WIDTH))
    return pl.pallas_call(
        _rope_kernel, grid=(t // tm,),
        in_specs=[src(COL_Q), src(COL_K), src(COL_V), tab, tab, tab],
        out_specs=pl.BlockSpec((tm, 3 * DA_WIDTH), lambda i: (i, 0)),
        out_shape=jax.ShapeDtypeStruct((t, 3 * DA_WIDTH), jnp.bfloat16),
        compiler_params=_params("parallel"), name="rope_cast")(proj, proj, proj, c, s1, s2)


def _attn_kernel(lam_ref, g_ref, q_ref, k_ref, v_ref, o_ref, m_ref, acc_ref, sa_ref, sb_ref, *, tk):
    nk = k_ref.shape[0] // tk
    m_ref[...] = jnp.full(m_ref.shape, NEG_INF, jnp.float32)
    acc_ref[...] = jnp.zeros(acc_ref.shape, jnp.float32)
    lane = lax.broadcasted_iota(jnp.int32, (tk, LANES), 1)
    ones_col = jnp.where(lane == 0, 1.0, 0.0).astype(jnp.bfloat16)

    def scores(j, s_ref):
        k = k_ref[pl.ds(pl.multiple_of(j * tk, tk), tk), :]
        for mp in range(2):
            q = q_ref[:, mp * DA_QK_DIM:(mp + 1) * DA_QK_DIM]
            s_ref[mp] = lax.dot_general(q, k[:, mp * DA_QK_DIM:(mp + 1) * DA_QK_DIM],
                                        (((1,), (1,)), ((), ())), preferred_element_type=jnp.float32)

    def accumulate(j, s_ref):
        vext = jnp.concatenate([v_ref[pl.ds(pl.multiple_of(j * tk, tk), tk), :], ones_col], axis=1)
        for mp in range(2):
            s = s_ref[mp]
            m_old = m_ref[mp]
            m_new = jnp.maximum(m_old, jnp.max(s, axis=-1, keepdims=True))
            p = jnp.exp(s - m_new).astype(jnp.bfloat16)
            pv = jnp.dot(p, vext, preferred_element_type=jnp.float32)
            acc_ref[mp] = jnp.exp(m_old - m_new) * acc_ref[mp] + pv
            m_ref[mp] = m_new

    scores(0, sa_ref)

    def body(jj, carry):
        j = 2 * jj
        scores(j + 1, sb_ref)
        accumulate(j, sa_ref)
        scores(j + 2, sa_ref)
        accumulate(j + 1, sb_ref)
        return carry

    lax.fori_loop(0, nk // 2 - 1, body, 0)
    scores(nk - 1, sb_ref)
    accumulate(nk - 2, sa_ref)
    accumulate(nk - 1, sb_ref)

    lp = lam_ref[...]
    lam = (jnp.exp(jnp.sum(lp[0:1] * lp[1:2], axis=-1, keepdims=True))
           - jnp.exp(jnp.sum(lp[2:3] * lp[3:4], axis=-1, keepdims=True)) + LAMBDA_INIT)
    a0, a1 = acc_ref[0], acc_ref[1]
    o = (a0[:, :DA_V_DIM] / a0[:, DA_V_DIM:DA_V_DIM + 1]
         - lam * (a1[:, :DA_V_DIM] / a1[:, DA_V_DIM:DA_V_DIM + 1]))
    y = o * lax.rsqrt(jnp.mean(o * o, axis=-1, keepdims=True) + LN_EPS)
    o_ref[...] = y * g_ref[...] * (1.0 - LAMBDA_INIT)


def _diff_attention(qkv, da_lambda, subln_g, batch, seq, tq, tk):
    t = qkv.shape[0]
    tq, tk = min(tq, seq), min(tk, seq // 2)
    assert (seq // tk) % 2 == 0, "the key loop is software-pipelined over chunk pairs"
    nq = seq // tq
    kern = functools.partial(_attn_kernel, tk=tk)
    return pl.pallas_call(
        kern, grid=(batch, DA_HEADS, nq),
        in_specs=[pl.BlockSpec((4, DA_QK_DIM), lambda b, h, i: (0, 0)),
                  pl.BlockSpec((1, DA_V_DIM), lambda b, h, i: (0, 0)),
                  pl.BlockSpec((tq, LANES), lambda b, h, i: (b * nq + i, h)),
                  pl.BlockSpec((seq, LANES), lambda b, h, i: (b, DA_HEADS + h)),
                  pl.BlockSpec((seq, LANES), lambda b, h, i: (b, 2 * DA_HEADS + h))],
        out_specs=pl.BlockSpec((tq, LANES), lambda b, h, i: (b * nq + i, h)),
        out_shape=jax.ShapeDtypeStruct((t, DA_WIDTH), jnp.float32),
        scratch_shapes=[pltpu.VMEM((2, tq, 1), jnp.float32),
                        pltpu.VMEM((2, tq, 2 * LANES), jnp.float32),
                        pltpu.VMEM((2, tq, tk), jnp.float32),
                        pltpu.VMEM((2, tq, tk), jnp.float32)],
        compiler_params=_params("parallel", "parallel", "parallel"),
        name="diff_attention")(da_lambda, subln_g.reshape(1, DA_V_DIM), qkv, qkv, qkv)


def _conv_kernel(prev_ref, cur_ref, next_ref, w_ref, b_ref, o_ref, *, nsb):
    i, c = pl.program_id(0), pl.program_id(1)
    tm = cur_ref.shape[0]
    sb = i % nsb
    prev = jnp.where(sb == 0, 0.0, prev_ref[...])
    nxt = jnp.where(sb == nsb - 1, 0.0, next_ref[...])
    ext = jnp.concatenate([prev, cur_ref[...], nxt], axis=0)
    n = tm + 16
    pad = (ML_CONV - 1) // 2
    acc = jnp.zeros((tm, cur_ref.shape[1]), jnp.float32) + b_ref[...]
    for w in range(ML_CONV):
        shift = (pad - w) % n
        z = ext if shift == 0 else pltpu.roll(ext, shift, 0)
        acc = acc + z[8:8 + tm] * w_ref[w:w + 1, :]
    y = acc * jax.nn.sigmoid(acc)
    y = y * jnp.where(c == 1, ML_HEAD_DIM ** -0.5, 1.0)
    o_ref[...] = y.astype(jnp.bfloat16)


def _conv_silu(proj, conv_w, conv_b, seq, tm):
    t = proj.shape[0]
    tm = min(tm, seq)
    nsb = seq // tm
    r8 = tm // 8
    nrow8 = t // 8
    cb0 = COL_MLQ // ML_WIDTH
    kern = functools.partial(_conv_kernel, nsb=nsb)
    return pl.pallas_call(
        kern, grid=(t // tm, 2),
        in_specs=[pl.BlockSpec((8, ML_WIDTH), lambda i, c: (jnp.maximum(i * r8 - 1, 0), cb0 + c)),
                  pl.BlockSpec((tm, ML_WIDTH), lambda i, c: (i, cb0 + c)),
                  pl.BlockSpec((8, ML_WIDTH), lambda i, c: (jnp.minimum((i + 1) * r8, nrow8 - 1), cb0 + c)),
                  pl.BlockSpec((ML_CONV, ML_WIDTH), lambda i, c: (0, c)),
                  pl.BlockSpec((1, ML_WIDTH), lambda i, c: (0, c))],
        out_specs=pl.BlockSpec((tm, ML_WIDTH), lambda i, c: (i, c)),
        out_shape=jax.ShapeDtypeStruct((t, 2 * ML_WIDTH), jnp.bfloat16),
        compiler_params=_params("parallel", "parallel"),
        name="conv_silu")(proj, proj, proj, conv_w, conv_b.reshape(1, 2 * ML_WIDTH))


def _seg_scan(x, pos, seq, op, fill, reverse):
    y = x
    d = 1
    while d < ML_CHUNK:
        if reverse:
            y = op(y, jnp.where(pos < ML_CHUNK - d, pltpu.roll(y, seq - d, 1), fill))
        else:
            y = op(y, jnp.where(pos >= d, pltpu.roll(y, d, 1), fill))
        d *= 2
    return y


def _gate_dir(i_pre, logf, pos, lane, seq, reverse):
    bj = _seg_scan(logf, pos, seq, jnp.add, 0.0, reverse)
    last = (pos == 0) if reverse else (pos == ML_CHUNK - 1)
    b_tot = _seg_scan(jnp.where(last, bj, NEG_INF), pos, seq, jnp.maximum, NEG_INF, not reverse)
    w = i_pre - bj
    cmax_w = _seg_scan(w, pos, seq, jnp.maximum, NEG_INF, reverse)
    g = b_tot - bj + i_pre
    g_max = jnp.maximum(_seg_scan(g, pos, seq, jnp.maximum, NEG_INF, False),
                        _seg_scan(g, pos, seq, jnp.maximum, NEG_INF, True))
    acc_a, acc_g = b_tot, g_max
    d = ML_CHUNK
    while d < seq:
        if reverse:
            ok = lane < seq - d
            pa = jnp.where(ok, pltpu.roll(acc_a, seq - d, 1), 0.0)
            pg = jnp.where(ok, pltpu.roll(acc_g, seq - d, 1), NEG_INF)
        else:
            ok = lane >= d
            pa = jnp.where(ok, pltpu.roll(acc_a, d, 1), 0.0)
            pg = jnp.where(ok, pltpu.roll(acc_g, d, 1), NEG_INF)
        acc_g = jnp.maximum(pg + acc_a, acc_g)
        acc_a = pa + acc_a
        d *= 2
    m_after = jnp.maximum(acc_a + STAB_INIT, acc_g)
    if reverse:
        m_before = jnp.where(lane < seq - ML_CHUNK, pltpu.roll(m_after, seq - ML_CHUNK, 1), STAB_INIT)
    else:
        m_before = jnp.where(lane >= ML_CHUNK, pltpu.roll(m_after, ML_CHUNK, 1), STAB_INIT)
    mr_rel = jnp.maximum(m_before, cmax_w)
    m_row = bj + mr_rel
    u = -mr_rel
    w_inter = jnp.exp(m_before - mr_rel)
    e_negm = jnp.exp(-m_row)
    wk = jnp.exp(g - m_after)
    a = jnp.exp(b_tot + m_before - m_after)
    return u, w, w_inter, e_negm, wk, a


def _gate_kernel(g_ref, b_ref, o_ref):
    seq = g_ref.shape[2]
    g = g_ref[0] + b_ref[...]
    i_pre, f_pre = g[0:8], g[8:16]
    logf = jnp.minimum(f_pre, 0.0) - jnp.log1p(jnp.exp(-jnp.abs(f_pre)))
    lane = lax.broadcasted_iota(jnp.int32, (2 * ML_HEADS, seq), 1)
    row = lax.broadcasted_iota(jnp.int32, (2 * ML_HEADS, seq), 0)
    pos = lane % ML_CHUNK
    fwd = _gate_dir(i_pre, logf, pos, lane, seq, False)
    bwd = _gate_dir(i_pre, logf, pos, lane, seq, True)
    for q, (af, ab) in enumerate(zip(fwd, bwd)):
        o_ref[0, q] = jnp.where(row < ML_HEADS, af, ab)


def _gate_scans(gates_t, gate_b):
    b, _, seq = gates_t.shape
    return pl.pallas_call(
        _gate_kernel, grid=(b,),
        in_specs=[pl.BlockSpec((1, 16, seq), lambda i: (i, 0, 0)),
                  pl.BlockSpec((16, 1), lambda i: (0, 0))],
        out_specs=pl.BlockSpec((1, 6, 8, seq), lambda i: (i, 0, 0, 0)),
        out_shape=jax.ShapeDtypeStruct((b, 6, 8, seq), jnp.float32),
        compiler_params=_params("parallel"), name="mlstm_gate_scans")(gates_t, gate_b.reshape(16, 1))


Q_U, Q_W, Q_WINTER, Q_ENEGM, Q_WK, Q_A = range(6)


def _mlstm_kernel(qf_ref, kf_ref, vf_ref, gf_ref, qb_ref, kb_ref, vb_ref, gb_ref,
                  of_ref, ob_ref, ce_ref, *, chunks):
    @pl.when(pl.program_id(1) == 0)
    def _():
        ce_ref[...] = jnp.zeros(ce_ref.shape, jnp.float32)

    L = ML_CHUNK
    row = lax.broadcasted_iota(jnp.int32, (L, L), 0)
    col = lax.broadcasted_iota(jnp.int32, (L, L), 1)
    lane = lax.broadcasted_iota(jnp.int32, (L, LANES), 1)
    ones_col = jnp.where(lane == 0, 1.0, 0.0).astype(jnp.bfloat16)
    dirs = ((0, qf_ref, kf_ref, vf_ref, gf_ref, of_ref, col <= row, range(chunks)),
            (1, qb_ref, kb_ref, vb_ref, gb_ref, ob_ref, col >= row, range(chunks - 1, -1, -1)))
    for d, q_ref, k_ref, v_ref, g_ref, o_ref, mask, order in dirs:
        n_q, n_r, rows = g_ref.shape[1:]
        cols = g_ref[0].reshape(n_q * n_r, rows).T
        for c in order:
            r0 = c * L
            for h in range(ML_HEADS):
                hs = slice(h * ML_HEAD_DIM, (h + 1) * ML_HEAD_DIM)
                cidx = lambda qn: cols[r0:r0 + L, qn * 8 + d * 4 + h:qn * 8 + d * 4 + h + 1]
                q = q_ref[r0:r0 + L, hs]
                k = k_ref[r0:r0 + L, hs]
                vext = jnp.concatenate([v_ref[r0:r0 + L, hs].astype(jnp.bfloat16), ones_col], axis=1)
                s = lax.dot_general(q, k, (((1,), (1,)), ((), ())), preferred_element_type=jnp.float32)
                w_row = g_ref[0, Q_W, d * 4 + h:d * 4 + h + 1, r0:r0 + L]
                dec = jnp.where(mask, jnp.exp(cidx(Q_U) + w_row), 0.0)
                sd = (s * dec).astype(jnp.bfloat16)
                ce = ce_ref[d * 4 + h]
                inter = jnp.dot(q, ce.astype(jnp.bfloat16), preferred_element_type=jnp.float32)
                intra = jnp.dot(sd, vext, preferred_element_type=jnp.float32)
                tot = cidx(Q_WINTER) * inter + intra
                den = jnp.maximum(jnp.abs(tot[:, ML_HEAD_DIM:ML_HEAD_DIM + 1]), cidx(Q_ENEGM))
                o_ref[r0:r0 + L, hs] = tot[:, :ML_HEAD_DIM] / den
                kw_t = (k.astype(jnp.float32) * cidx(Q_WK)).T.astype(jnp.bfloat16)
                upd = jnp.dot(kw_t, vext, preferred_element_type=jnp.float32)
                a = cols[r0:r0 + 1, Q_A * 8 + d * 4 + h:Q_A * 8 + d * 4 + h + 1]
                ce_ref[d * 4 + h] = a * ce + upd


def _mlstm_scan(qk_conv, proj, gtab, batch, seq, chunks):
    t = qk_conv.shape[0]
    rows = chunks * ML_CHUNK
    ns = seq // rows
    vcb = COL_MLV // ML_WIDTH
    fw = lambda b, j: b * ns + j
    bw = lambda b, j: b * ns + ns - 1 - j
    def specs(rb, lb):
        return [pl.BlockSpec((rows, ML_WIDTH), lambda b, j: (rb(b, j), 0)),
                pl.BlockSpec((rows, ML_WIDTH), lambda b, j: (rb(b, j), 1)),
                pl.BlockSpec((rows, ML_WIDTH), lambda b, j: (rb(b, j), vcb)),
                pl.BlockSpec((1,) + gtab.shape[1:3] + (rows,), lambda b, j: (b, 0, 0, lb(j)))]
    out_f = pl.BlockSpec((rows, ML_WIDTH), lambda b, j: (fw(b, j), 0))
    out_b = pl.BlockSpec((rows, ML_WIDTH), lambda b, j: (bw(b, j), 0))
    kern = functools.partial(_mlstm_kernel, chunks=chunks)
    return pl.pallas_call(
        kern, grid=(batch, ns),
        in_specs=specs(fw, lambda j: j) + specs(bw, lambda j: ns - 1 - j),
        out_specs=[out_f, out_b],
        out_shape=[jax.ShapeDtypeStruct((t, ML_WIDTH), jnp.float32)] * 2,
        scratch_shapes=[pltpu.VMEM((2 * ML_HEADS, ML_HEAD_DIM, 2 * LANES), jnp.float32)],
        compiler_params=_params("parallel", "arbitrary"),
        name="mlstm_scan")(qk_conv, qk_conv, proj, gtab, qk_conv, qk_conv, proj, gtab)


def _layer_norm_rows(z, g, b):
    mu = jnp.mean(z, axis=-1, keepdims=True)
    zc = z - mu
    var = jnp.mean(zc * zc, axis=-1, keepdims=True)
    return zc * lax.rsqrt(var + LN_EPS) * g + b


def _merge_kernel(x_ref, ya_ref, hf_ref, hb_ref, mo_ref, g0_ref, g1_ref, ng_ref,
                  wa_ref, wm_ref, wo_ref, lg_ref, lb_ref, o_ref, ot_ref, otb_ref):
    h = hf_ref[...] + hb_ref[...]
    parts = []
    for hd in range(ML_HEADS):
        hh = h[:, hd * ML_HEAD_DIM:(hd + 1) * ML_HEAD_DIM]
        mu = jnp.mean(hh, axis=-1, keepdims=True)
        hc = hh - mu
        var = jnp.mean(hc * hc, axis=-1, keepdims=True)
        parts.append(hc * lax.rsqrt(var + LN_EPS))
    ym = jnp.concatenate(parts, axis=1) * ng_ref[...] * jax.nn.sigmoid(mo_ref[...])
    pa = jnp.dot(ya_ref[...].astype(jnp.bfloat16), wa_ref[...], preferred_element_type=jnp.float32)
    pm = jnp.dot(ym.astype(jnp.bfloat16), wm_ref[...], preferred_element_type=jnp.float32)
    merged = jax.nn.sigmoid(g0_ref[...]) * pa + jax.nn.sigmoid(g1_ref[...]) * pm
    mix = jnp.dot(merged.astype(jnp.bfloat16), wo_ref[...], preferred_element_type=jnp.float32)
    x1 = _layer_norm_rows(DN_ALPHA * x_ref[...] + mix, lg_ref[...], lb_ref[...])
    o_ref[...] = x1
    x1t = x1.T
    ot_ref[...] = x1t
    otb_ref[...] = x1t.astype(jnp.bfloat16)


def _merge(xt, y_attn, h_fw, h_bw, proj, norm_g, wa, wm, wo, ln_g, ln_b, tm):
    t = xt.shape[0]
    tm = min(tm, t)
    row = lambda w, cb: pl.BlockSpec((tm, w), lambda i: (i, cb))
    full = lambda a: pl.BlockSpec(a.shape, lambda i: (0, 0))
    vec = lambda a: a.reshape(1, -1)
    args = (xt, y_attn, h_fw, h_bw, proj, proj, proj, vec(norm_g), wa, wm, wo, vec(ln_g), vec(ln_b))
    in_specs = [row(D_MODEL, 0), row(DA_WIDTH, 0), row(ML_WIDTH, 0), row(ML_WIDTH, 0),
                row(ML_WIDTH, COL_MLO // ML_WIDTH),
                row(D_MODEL, COL_BR // D_MODEL), row(D_MODEL, COL_BR // D_MODEL + 1)]
    in_specs += [full(a) for a in args[7:]]
    return pl.pallas_call(
        _merge_kernel, grid=(t // tm,), in_specs=in_specs,
        out_specs=[pl.BlockSpec((tm, D_MODEL), lambda i: (i, 0)),
                   pl.BlockSpec((D_MODEL, tm), lambda i: (0, i)),
                   pl.BlockSpec((D_MODEL, tm), lambda i: (0, i))],
        out_shape=[jax.ShapeDtypeStruct((t, D_MODEL), jnp.float32),
                   jax.ShapeDtypeStruct((D_MODEL, t), jnp.float32),
                   jax.ShapeDtypeStruct((D_MODEL, t), jnp.bfloat16)],
        compiler_params=_params("parallel"), name="merge_ln")(*args)


NOT_TOP = float(PEER_TOPK)
N_CAND_ROWS = 80


def _topk_rows(s, dst_ref, want_rank):
    rank = jnp.full(s.shape, NOT_TOP, jnp.float32) if want_rank else None
    for x in range(PEER_TOPK):
        m = jnp.max(s, axis=0, keepdims=True)
        dst_ref[x:x + 1, :] = m
        hit = s == m
        if want_rank:
            rank = jnp.where(hit, float(x), rank)
        s = jnp.where(hit, NEG_INF, s)
    return rank


def _peer_prep_kernel(q_ref, kh_ref, kl_ref, ry_ref, cnt_ref, amp_ref, eb_ref, a_ref, b_ref, sum_ref):
    for head in range(PEER_HEADS):
        _peer_prep_head(head, q_ref, kh_ref, kl_ref, ry_ref, cnt_ref, amp_ref, eb_ref, a_ref, b_ref, sum_ref)


def _peer_prep_head(head, q_ref, kh_ref, kl_ref, ry_ref, cnt_ref, amp_ref, eb_ref, a_ref, b_ref, sum_ref):
    half = N_KEYS
    sc = []
    for p in range(2):
        col = (2 * head + p) * half
        qh, ql = _split_bf16(q_ref[:, col:col + half])
        sc.append(_dot3(kh_ref[p], kl_ref[p], qh, ql, ((1,), (1,))))
    s1, s2 = sc
    _topk_rows(s1, a_ref, False)
    rank2 = _topk_rows(s2, b_ref, True)
    a, b = a_ref[...], b_ref[...]
    h8 = PEER_TOPK // 2
    sum_ref[0:PEER_TOPK, :] = a + b[0:1, :]
    for y in range(1, h8):
        sum_ref[PEER_TOPK + (y - 1) * h8:PEER_TOPK + y * h8, :] = a[0:h8, :] + b[y:y + 1, :]
    sum_ref[N_CAND_ROWS - h8:N_CAND_ROWS, :] = b[h8:PEER_TOPK, :] + a[0:1, :]
    sums = sum_ref[...]
    rest = sums
    for _ in range(PEER_TOPK):
        tau = jnp.max(rest, axis=0, keepdims=True)
        rest = jnp.where(rest == tau, NEG_INF, rest)
    top = a[0:1, :] + b[0:1, :]
    z = jnp.sum(jnp.where(sums >= tau, jnp.exp(sums - top), 0.0), axis=0, keepdims=True)
    cntx = jnp.zeros(a.shape, jnp.float32)
    for y in range(PEER_TOPK):
        cntx = cntx + jnp.where(a + b[y:y + 1, :] >= tau, 1.0, 0.0)
    cnt = jnp.zeros(s1.shape, jnp.float32)
    for x in range(PEER_TOPK):
        cnt = jnp.where(s1 == a[x:x + 1, :], cntx[x:x + 1, :], cnt)
    results = ((ry_ref, rank2.astype(jnp.bfloat16)),
               (cnt_ref, cnt),
               (amp_ref, jnp.exp(s1 - a[0:1, :])),
               (eb_ref, (jnp.exp(s2 - b[0:1, :]) / z).astype(jnp.bfloat16)))
    for ref, val in results:
        for tc in range(ref.shape[1]):
            ref[head, tc] = val[:, tc * LANES:(tc + 1) * LANES]


def _peer_prep(q, keys_hi, keys_lo, tt):
    t = q.shape[0]
    tt = min(tt, t)
    qdim = PEER_HEADS * 2 * N_KEYS
    shape = (PEER_HEADS, t // LANES, N_KEYS, LANES)
    outs = [jax.ShapeDtypeStruct(shape, dt) for dt in (jnp.bfloat16, jnp.float32, jnp.float32, jnp.bfloat16)]
    ospec = pl.BlockSpec((PEER_HEADS, tt // LANES, N_KEYS, LANES), lambda i: (0, i, 0, 0))
    kspec = pl.BlockSpec((2, N_KEYS, N_KEYS), lambda i: (0, 0, 0))
    return pl.pallas_call(
        _peer_prep_kernel, grid=(t // tt,),
        in_specs=[pl.BlockSpec((tt, qdim), lambda i: (i, 0)), kspec, kspec],
        out_specs=[ospec] * 4, out_shape=outs,
        scratch_shapes=[pltpu.VMEM((PEER_TOPK, tt), jnp.float32),
                        pltpu.VMEM((PEER_TOPK, tt), jnp.float32),
                        pltpu.VMEM((N_CAND_ROWS, tt), jnp.float32)],
        compiler_params=_params("parallel"), name="peer_prep")(q, keys_hi, keys_lo)


PEER_PIPE_LAG = 1


def _peer_main_kernel(xb_ref, xf_ref, u_ref, vt_ref, ry_ref, cnt_ref, amp_ref, eb_ref, lg_ref, lb_ref,
                      o_ref, acc_ref, pre0_ref, pre1_ref, *, rows_per_step, n_blocks):
    e = pl.program_id(1)
    tb = xb_ref.shape[1]
    bf = jnp.bfloat16

    @pl.when(e == 0)
    def _():
        acc_ref[...] = jnp.zeros(acc_ref.shape, jnp.float32)
        pre0_ref[...] = jnp.zeros(pre0_ref.shape, jnp.float32)
        pre1_ref[...] = jnp.zeros(pre1_ref.shape, jnp.float32)

    blk = jnp.clip(e - 1, 0, n_blocks - 1)

    def weigh(ii, pre_r):
        i = blk * rows_per_step + ii
        rows = slice(ii * N_KEYS, (ii + 1) * N_KEYS)
        zero = jnp.zeros((N_KEYS, LANES), bf)
        tiles = []
        for tc in range(tb // LANES):
            cols = slice(tc * LANES, (tc + 1) * LANES)
            spread = lambda ref, h: jnp.broadcast_to(
                ref[h, tc, pl.ds(i, 1), :], (N_KEYS, LANES)).astype(bf)
            g = zero
            for h in range(PEER_HEADS):
                sel = jnp.maximum(jnp.minimum(spread(cnt_ref, h) - ry_ref[h, tc], eb_ref[h, tc]), zero)
                g = g + spread(amp_ref, h) * sel
            pre = pre_r[rows, cols]
            act = 0.5 * pre * (1.0 + lax.erf(pre * (2.0 ** -0.5)))
            tiles.append(g * act.astype(bf))
        return jnp.concatenate(tiles, axis=1)

    def step(pre_w, pre_r):
        wgt = jnp.concatenate([weigh(ii, pre_r) for ii in range(rows_per_step)], axis=0)
        pre_w[...] = jnp.dot(u_ref[...], xb_ref[...], preferred_element_type=jnp.float32)
        acc_ref[...] += jnp.dot(vt_ref[...], wgt, preferred_element_type=jnp.float32)

    for par in range(2):
        @pl.when(e % 2 == par)
        def _():
            pre = (pre0_ref, pre1_ref)
            step(pre[par], pre[1 - par])

    @pl.when(e == n_blocks + PEER_PIPE_LAG - 1)
    def _():
        z = DN_ALPHA * xf_ref[...] + acc_ref[...]
        mu = jnp.mean(z, axis=0, keepdims=True)
        zc = z - mu
        var = jnp.mean(zc * zc, axis=0, keepdims=True)
        o_ref[...] = (zc * lax.rsqrt(var + LN_EPS)).T * lg_ref[...] + lb_ref[...]


def _peer_main(x1t_bf, x1t, u_bf, vt_bf, ry, cnt, amp, eb, ln_g, ln_b, tb, eb_rows):
    d, t = x1t.shape
    tb = min(tb, t)
    rows_per_step = eb_rows // N_KEYS
    n_blocks = N_EXPERTS // eb_rows
    tok = lambda: pl.BlockSpec((d, tb), lambda i, e: (0, i))
    tab = lambda: pl.BlockSpec((PEER_HEADS, tb // LANES, N_KEYS, LANES), lambda i, e: (0, i, 0, 0))
    colv = lambda: pl.BlockSpec((1, d), lambda i, e: (0, 0))
    kern = functools.partial(_peer_main_kernel, rows_per_step=rows_per_step, n_blocks=n_blocks)
    return pl.pallas_call(
        kern, grid=(t // tb, n_blocks + PEER_PIPE_LAG),
        in_specs=[tok(), tok(),
                  pl.BlockSpec((eb_rows, d), lambda i, e: (jnp.minimum(e, n_blocks - 1), 0)),
                  pl.BlockSpec((d, eb_rows), lambda i, e: (0, jnp.clip(e - PEER_PIPE_LAG, 0, n_blocks - 1))),
                  tab(), tab(), tab(), tab(), colv(), colv()],
        out_specs=pl.BlockSpec((tb, d), lambda i, e: (i, 0)),
        out_shape=jax.ShapeDtypeStruct((t, d), jnp.float32),
        scratch_shapes=[pltpu.VMEM((d, tb), jnp.float32),
                        pltpu.VMEM((eb_rows, tb), jnp.float32),
                        pltpu.VMEM((eb_rows, tb), jnp.float32)],
        compiler_params=_params("parallel", "arbitrary"),
        name="peer_dense")(x1t_bf, x1t, u_bf, vt_bf, ry, cnt, amp, eb,
                           ln_g.reshape(1, d), ln_b.reshape(1, d))


def _token_mixing(xt, batch, seq, w_in, da_lambda, da_subln_g, ml_conv_w, ml_conv_b, ml_gate_b,
                  ml_norm_g, w_branch_attn, w_branch_mlstm, w_out, ln1_g, ln1_b):
    t = batch * seq
    bf = jnp.bfloat16
    zpad = jnp.zeros((D_MODEL, LANES - N_GATE_COLS), jnp.float32)
    w_cols = jnp.concatenate([w_in[:, IN_MAIN + N_GATE_COLS:], w_in[:, :IN_MAIN],
                              w_in[:, IN_MAIN:IN_MAIN + N_GATE_COLS], zpad], axis=1).astype(bf)
    proj = _matmul(xt, w_cols, 512, 1152)

    qkv = _rope_cast(proj, seq, 512)
    y_attn = _diff_attention(qkv, da_lambda, da_subln_g, batch, seq, 512, 1024)

    qk_conv = _conv_silu(proj, ml_conv_w, ml_conv_b, seq, 512)
    order = jnp.array([0, 2, 1, 3])
    gates_t = proj[:, COL_MLG:COL_MLG + N_GATE_COLS].reshape(batch, seq, 4, ML_HEADS)[:, :, order]
    gates_t = gates_t.reshape(batch, seq, N_GATE_COLS).transpose(0, 2, 1)
    gtab = _gate_scans(gates_t, ml_gate_b[order].reshape(N_GATE_COLS))
    h_fw, h_bw = _mlstm_scan(qk_conv, proj, gtab, batch, seq, min(4, seq // ML_CHUNK))

    return _merge(xt, y_attn, h_fw, h_bw, proj, ml_norm_g,
                  w_branch_attn.astype(bf), w_branch_mlstm.astype(bf), w_out.astype(bf),
                  ln1_g, ln1_b, 256)


def _peer_layer(x1, x1t, x1t_bf, peer_w_q, peer_sub_keys, peer_u, peer_v, ln2_g, ln2_b):
    bf = jnp.bfloat16
    wq_hi, wq_lo = _split_bf16(peer_w_q)
    q = _matmul3(x1, wq_hi, wq_lo, 512, 512)
    k_hi, k_lo = _split_bf16(peer_sub_keys)
    ry, cnt, amp, eb = _peer_prep(q, k_hi, k_lo, 256)
    return _peer_main(x1t_bf, x1t, peer_u.astype(bf), peer_v.T.astype(bf),
                      ry, cnt, amp, eb, ln2_g, ln2_b, 512, 1024)


def kernel(x, w_in, da_lambda, da_subln_g, ml_conv_w, ml_conv_b, ml_gate_b, ml_norm_g,
           w_branch_attn, w_branch_mlstm, w_out, ln1_g, ln1_b,
           peer_w_q, peer_sub_keys, peer_u, peer_v, ln2_g, ln2_b):
    batch, seq, d = x.shape
    xt = x.reshape(batch * seq, d)
    x1, x1t, x1t_bf = _token_mixing(xt, batch, seq, w_in[0], da_lambda[0], da_subln_g[0], ml_conv_w[0],
                                    ml_conv_b[0], ml_gate_b[0], ml_norm_g[0], w_branch_attn[0],
                                    w_branch_mlstm[0], w_out[0], ln1_g[0], ln1_b[0])
    out = _peer_layer(x1, x1t, x1t_bf, peer_w_q[0], peer_sub_keys[0], peer_u[0], peer_v[0],
                      ln2_g[0], ln2_b[0])
    return out.reshape(batch, seq, d)
```

```python
import functools
import math

import jax
import jax.numpy as jnp
from jax import lax
from jax.experimental import pallas as pl
from jax.experimental.pallas import tpu as pltpu

D_MODEL = 1024
DA_QK_DIM = 64
DA_V_DIM = 128
DA_HEADS = 4
DA_WIDTH = 512
ROPE_DIM = 16
ROPE_THETA = 500000.0
ML_WIDTH = 512
ML_HEADS = 4
ML_HEAD_DIM = 128
ML_CONV = 5
ML_CHUNK = 64
STAB_INIT = -1e30
N_KEYS = 128
N_EXPERTS = N_KEYS * N_KEYS
PEER_HEADS = 8
PEER_TOPK = 16
DEPTH = 1
DN_ALPHA = (2 * DEPTH) ** 0.25
LN_EPS = 1e-5
LAMBDA_INIT = 0.8 - 0.6 * math.exp(-0.3 * 0)

LANES = 128
NEG_INF = float("-inf")
VMEM_LIMIT = 48 * 1024 * 1024

TILE_PROJ = (2048, 1152)
TILE_ROWS = 2048
TILE_ATTN = (512, 1024)
MLSTM_CHUNKS = 4
TILE_MERGE = 512
TILE_QPROJ = (1024, 1024)
TILE_PREP = 256
TILE_PREACT = (1024, 1024)
TILE_PEER = (512, 2048)

COL_BR = 0
COL_Q, COL_K, COL_V = 2048, 2560, 3072
COL_MLQ, COL_MLK, COL_MLV, COL_MLO = 3584, 4096, 4608, 5120
COL_MLG = 5632
N_GATE_COLS = 4 * ML_HEADS
IN_MAIN = 3584


def _params(*sem):
    return pltpu.CompilerParams(dimension_semantics=sem, vmem_limit_bytes=VMEM_LIMIT)


def _mm_kernel(x_ref, w_ref, o_ref):
    o_ref[...] = jnp.dot(x_ref[...].astype(jnp.bfloat16), w_ref[...],
                         preferred_element_type=jnp.float32).astype(o_ref.dtype)


def _matmul(x, w, tm, tn, name, out_dtype=jnp.float32):
    m, k = x.shape
    n = w.shape[1]
    tm, tn = min(tm, m), min(tn, n)
    return pl.pallas_call(
        _mm_kernel, grid=(m // tm, n // tn),
        in_specs=[pl.BlockSpec((tm, k), lambda i, j: (i, 0)),
                  pl.BlockSpec((k, tn), lambda i, j: (0, j))],
        out_specs=pl.BlockSpec((tm, tn), lambda i, j: (i, j)),
        out_shape=jax.ShapeDtypeStruct((m, n), out_dtype),
        compiler_params=_params("parallel", "parallel"), name=name)(x, w)


def _split_bf16(a):
    hi = a.astype(jnp.bfloat16)
    lo = (a - hi.astype(jnp.float32)).astype(jnp.bfloat16)
    return hi, lo


def _dot3(a_hi, a_lo, b_hi, b_lo, dims):
    f = lambda a, b: lax.dot_general(a, b, (dims, ((), ())), preferred_element_type=jnp.float32)
    return f(a_hi, b_hi) + f(a_hi, b_lo) + f(a_lo, b_hi)


def _mm3_kernel(x_ref, wh_ref, wl_ref, o_ref):
    xh, xl = _split_bf16(x_ref[...])
    o_ref[...] = _dot3(xh, xl, wh_ref[...], wl_ref[...], ((1,), (0,)))


def _matmul3(x, w_hi, w_lo, tm, tn):
    m, k = x.shape
    n = w_hi.shape[1]
    tm, tn = min(tm, m), min(tn, n)
    return pl.pallas_call(
        _mm3_kernel, grid=(m // tm, n // tn),
        in_specs=[pl.BlockSpec((tm, k), lambda i, j: (i, 0)),
                  pl.BlockSpec((k, tn), lambda i, j: (0, j)),
                  pl.BlockSpec((k, tn), lambda i, j: (0, j))],
        out_specs=pl.BlockSpec((tm, tn), lambda i, j: (i, j)),
        out_shape=jax.ShapeDtypeStruct((m, n), jnp.float32),
        compiler_params=_params("parallel", "parallel"), name="dense_matmul3")(x, w_hi, w_lo)


def _rope_kernel(q_ref, k_ref, v_ref, c_ref, s1_ref, s2_ref, o_ref):
    c, s1, s2 = c_ref[...], s1_ref[...], s2_ref[...]
    for which, src in enumerate((q_ref, k_ref)):
        for hd in range(DA_HEADS):
            x = src[:, hd * LANES:(hd + 1) * LANES]
            r = x * c + pltpu.roll(x, LANES - 8, 1) * s1 + pltpu.roll(x, 8, 1) * s2
            if which == 0:
                r = r * (DA_QK_DIM ** -0.5)
            col = which * DA_WIDTH + hd * LANES
            o_ref[:, col:col + LANES] = r.astype(jnp.bfloat16)
    o_ref[:, 2 * DA_WIDTH:3 * DA_WIDTH] = v_ref[...].astype(jnp.bfloat16)


def _rope_tables(seq):
    pos = jnp.arange(seq, dtype=jnp.float32)
    inv_freq = ROPE_THETA ** (-jnp.arange(0, ROPE_DIM, 2, dtype=jnp.float32) / ROPE_DIM)
    ang = pos[:, None] * inv_freq[None, :]
    cos, sin = jnp.cos(ang), jnp.sin(ang)
    half = ROPE_DIM // 2
    zeros = jnp.zeros((seq, DA_QK_DIM - ROPE_DIM), jnp.float32)
    z8 = jnp.zeros((seq, half), jnp.float32)
    c64 = jnp.concatenate([cos, cos, zeros + 1.0], -1)
    s1_64 = jnp.concatenate([-sin, z8, zeros], -1)
    s2_64 = jnp.concatenate([z8, sin, zeros], -1)
    tile2 = lambda t: jnp.concatenate([t, t], -1)
    return tile2(c64), tile2(s1_64), tile2(s2_64)


def _rope_cast(proj, seq, tm):
    t = proj.shape[0]
    tm = min(tm, seq)
    nsb = seq // tm
    c, s1, s2 = _rope_tables(seq)
    tab = pl.BlockSpec((tm, LANES), lambda i: (i % nsb, 0))
    src = lambda col: pl.BlockSpec((tm, DA_WIDTH), lambda i: (i, col // DA_WIDTH))
    return pl.pallas_call(
        _rope_kernel, grid=(t // tm,),
        in_specs=[src(COL_Q), src(COL_K), src(COL_V), tab, tab, tab],
        out_specs=pl.BlockSpec((tm, 3 * DA_WIDTH), lambda i: (i, 0)),
        out_shape=jax.ShapeDtypeStruct((t, 3 * DA_WIDTH), jnp.bfloat16),
        compiler_params=_params("parallel"), name="rope_cast")(proj, proj, proj, c, s1, s2)


def _attn_kernel(lam_ref, g_ref, q_ref, k_ref, v_ref, o_ref, m_ref, acc_ref, sa_ref, sb_ref, *, tk):
    nk = k_ref.shape[0] // tk
    m_ref[...] = jnp.full(m_ref.shape, NEG_INF, jnp.float32)
    acc_ref[...] = jnp.zeros(acc_ref.shape, jnp.float32)
    lane = lax.broadcasted_iota(jnp.int32, (tk, LANES), 1)
    ones_col = jnp.where(lane == 0, 1.0, 0.0).astype(jnp.bfloat16)

    def scores(j, s_ref):
        k = k_ref[pl.ds(pl.multiple_of(j * tk, tk), tk), :]
        for mp in range(2):
            q = q_ref[:, mp * DA_QK_DIM:(mp + 1) * DA_QK_DIM]
            s_ref[mp] = lax.dot_general(q, k[:, mp * DA_QK_DIM:(mp + 1) * DA_QK_DIM],
                                        (((1,), (1,)), ((), ())), preferred_element_type=jnp.float32)

    def accumulate(j, s_ref):
        vext = jnp.concatenate([v_ref[pl.ds(pl.multiple_of(j * tk, tk), tk), :], ones_col], axis=1)
        for mp in range(2):
            s = s_ref[mp]
            m_old = m_ref[mp]
            m_new = jnp.maximum(m_old, jnp.max(s, axis=-1, keepdims=True))
            p = jnp.exp(s - m_new).astype(jnp.bfloat16)
            pv = jnp.dot(p, vext, preferred_element_type=jnp.float32)
            acc_ref[mp] = jnp.exp(m_old - m_new) * acc_ref[mp] + pv
            m_ref[mp] = m_new

    scores(0, sa_ref)

    def body(jj, carry):
        j = 2 * jj
        scores(j + 1, sb_ref)
        accumulate(j, sa_ref)
        scores(j + 2, sa_ref)
        accumulate(j + 1, sb_ref)
        return carry

    lax.fori_loop(0, nk // 2 - 1, body, 0)
    scores(nk - 1, sb_ref)
    accumulate(nk - 2, sa_ref)
    accumulate(nk - 1, sb_ref)

    lp = lam_ref[...]
    lam = (jnp.exp(jnp.sum(lp[0:1] * lp[1:2], axis=-1, keepdims=True))
           - jnp.exp(jnp.sum(lp[2:3] * lp[3:4], axis=-1, keepdims=True)) + LAMBDA_INIT)
    a0, a1 = acc_ref[0], acc_ref[1]
    o = (a0[:, :DA_V_DIM] / a0[:, DA_V_DIM:DA_V_DIM + 1]
         - lam * (a1[:, :DA_V_DIM] / a1[:, DA_V_DIM:DA_V_DIM + 1]))
    y = o * lax.rsqrt(jnp.mean(o * o, axis=-1, keepdims=True) + LN_EPS)
    o_ref[...] = y * g_ref[...] * (1.0 - LAMBDA_INIT)


def _diff_attention(qkv, da_lambda, subln_g, batch, seq, tq, tk):
    t = qkv.shape[0]
    tq, tk = min(tq, seq), min(tk, seq // 2)
    assert (seq // tk) % 2 == 0, "the key loop is software-pipelined over chunk pairs"
    nq = seq // tq
    kern = functools.partial(_attn_kernel, tk=tk)
    return pl.pallas_call(
        kern, grid=(batch, DA_HEADS, nq),
        in_specs=[pl.BlockSpec((4, DA_QK_DIM), lambda b, h, i: (0, 0)),
                  pl.BlockSpec((1, DA_V_DIM), lambda b, h, i: (0, 0)),
                  pl.BlockSpec((tq, LANES), lambda b, h, i: (b * nq + i, h)),
                  pl.BlockSpec((seq, LANES), lambda b, h, i: (b, DA_HEADS + h)),
                  pl.BlockSpec((seq, LANES), lambda b, h, i: (b, 2 * DA_HEADS + h))],
        out_specs=pl.BlockSpec((tq, LANES), lambda b, h, i: (b * nq + i, h)),
        out_shape=jax.ShapeDtypeStruct((t, DA_WIDTH), jnp.float32),
        scratch_shapes=[pltpu.VMEM((2, tq, 1), jnp.float32),
                        pltpu.VMEM((2, tq, 2 * LANES), jnp.float32),
                        pltpu.VMEM((2, tq, tk), jnp.float32),
                        pltpu.VMEM((2, tq, tk), jnp.float32)],
        compiler_params=_params("parallel", "parallel", "parallel"),
        name="diff_attention")(da_lambda, subln_g.reshape(1, DA_V_DIM), qkv, qkv, qkv)


def _conv_kernel(prev_ref, cur_ref, next_ref, w_ref, b_ref, o_ref, *, nsb):
    i, c = pl.program_id(0), pl.program_id(1)
    tm = cur_ref.shape[0]
    sb = i % nsb
    prev = jnp.where(sb == 0, 0.0, prev_ref[...])
    nxt = jnp.where(sb == nsb - 1, 0.0, next_ref[...])
    ext = jnp.concatenate([prev, cur_ref[...], nxt], axis=0)
    n = tm + 16
    pad = (ML_CONV - 1) // 2
    acc = jnp.zeros((tm, cur_ref.shape[1]), jnp.float32) + b_ref[...]
    for w in range(ML_CONV):
        shift = (pad - w) % n
        z = ext if shift == 0 else pltpu.roll(ext, shift, 0)
        acc = acc + z[8:8 + tm] * w_ref[w:w + 1, :]
    y = acc * jax.nn.sigmoid(acc)
    y = y * jnp.where(c == 1, ML_HEAD_DIM ** -0.5, 1.0)
    o_ref[...] = y.astype(jnp.bfloat16)


def _conv_silu(proj, conv_w, conv_b, seq, tm):
    t = proj.shape[0]
    tm = min(tm, seq)
    nsb = seq // tm
    r8 = tm // 8
    nrow8 = t // 8
    cb0 = COL_MLQ // ML_WIDTH
    kern = functools.partial(_conv_kernel, nsb=nsb)
    return pl.pallas_call(
        kern, grid=(t // tm, 2),
        in_specs=[pl.BlockSpec((8, ML_WIDTH), lambda i, c: (jnp.maximum(i * r8 - 1, 0), cb0 + c)),
                  pl.BlockSpec((tm, ML_WIDTH), lambda i, c: (i, cb0 + c)),
                  pl.BlockSpec((8, ML_WIDTH), lambda i, c: (jnp.minimum((i + 1) * r8, nrow8 - 1), cb0 + c)),
                  pl.BlockSpec((ML_CONV, ML_WIDTH), lambda i, c: (0, c)),
                  pl.BlockSpec((1, ML_WIDTH), lambda i, c: (0, c))],
        out_specs=pl.BlockSpec((tm, ML_WIDTH), lambda i, c: (i, c)),
        out_shape=jax.ShapeDtypeStruct((t, 2 * ML_WIDTH), jnp.bfloat16),
        compiler_params=_params("parallel", "parallel"),
        name="conv_silu")(proj, proj, proj, conv_w, conv_b.reshape(1, 2 * ML_WIDTH))


def _seg_scan(x, pos, seq, op, fill, reverse):
    y = x
    d = 1
    while d < ML_CHUNK:
        if reverse:
            y = op(y, jnp.where(pos < ML_CHUNK - d, pltpu.roll(y, seq - d, 1), fill))
        else:
            y = op(y, jnp.where(pos >= d, pltpu.roll(y, d, 1), fill))
        d *= 2
    return y


def _gate_dir(i_pre, logf, pos, lane, seq, reverse):
    bj = _seg_scan(logf, pos, seq, jnp.add, 0.0, reverse)
    last = (pos == 0) if reverse else (pos == ML_CHUNK - 1)
    b_tot = _seg_scan(jnp.where(last, bj, NEG_INF), pos, seq, jnp.maximum, NEG_INF, not reverse)
    w = i_pre - bj
    cmax_w = _seg_scan(w, pos, seq, jnp.maximum, NEG_INF, reverse)
    g = b_tot - bj + i_pre
    g_max = jnp.maximum(_seg_scan(g, pos, seq, jnp.maximum, NEG_INF, False),
                        _seg_scan(g, pos, seq, jnp.maximum, NEG_INF, True))
    acc_a, acc_g = b_tot, g_max
    d = ML_CHUNK
    while d < seq:
        if reverse:
            ok = lane < seq - d
            pa = jnp.where(ok, pltpu.roll(acc_a, seq - d, 1), 0.0)
            pg = jnp.where(ok, pltpu.roll(acc_g, seq - d, 1), NEG_INF)
        else:
            ok = lane >= d
            pa = jnp.where(ok, pltpu.roll(acc_a, d, 1), 0.0)
            pg = jnp.where(ok, pltpu.roll(acc_g, d, 1), NEG_INF)
        acc_g = jnp.maximum(pg + acc_a, acc_g)
        acc_a = pa + acc_a
        d *= 2
    m_after = jnp.maximum(acc_a + STAB_INIT, acc_g)
    if reverse:
        m_before = jnp.where(lane < seq - ML_CHUNK, pltpu.roll(m_after, seq - ML_CHUNK, 1), STAB_INIT)
    else:
        m_before = jnp.where(lane >= ML_CHUNK, pltpu.roll(m_after, ML_CHUNK, 1), STAB_INIT)
    mr_rel = jnp.maximum(m_before, cmax_w)
    m_row = bj + mr_rel
    u = -mr_rel
    w_inter = jnp.exp(m_before - mr_rel)
    e_negm = jnp.exp(-m_row)
    wk = jnp.exp(g - m_after)
    a = jnp.exp(b_tot + m_before - m_after)
    return u, w, w_inter, e_negm, wk, a


def _gate_kernel(g_ref, b_ref, o_ref):
    seq = g_ref.shape[2]
    g = g_ref[0] + b_ref[...]
    i_pre, f_pre = g[0:8], g[8:16]
    logf = jnp.minimum(f_pre, 0.0) - jnp.log1p(jnp.exp(-jnp.abs(f_pre)))
    lane = lax.broadcasted_iota(jnp.int32, (2 * ML_HEADS, seq), 1)
    row = lax.broadcasted_iota(jnp.int32, (2 * ML_HEADS, seq), 0)
    pos = lane % ML_CHUNK
    fwd = _gate_dir(i_pre, logf, pos, lane, seq, False)
    bwd = _gate_dir(i_pre, logf, pos, lane, seq, True)
    for q, (af, ab) in enumerate(zip(fwd, bwd)):
        o_ref[0, q] = jnp.where(row < ML_HEADS, af, ab)


def _gate_scans(gates_t, gate_b):
    b, _, seq = gates_t.shape
    return pl.pallas_call(
        _gate_kernel, grid=(b,),
        in_specs=[pl.BlockSpec((1, 16, seq), lambda i: (i, 0, 0)),
                  pl.BlockSpec((16, 1), lambda i: (0, 0))],
        out_specs=pl.BlockSpec((1, 6, 8, seq), lambda i: (i, 0, 0, 0)),
        out_shape=jax.ShapeDtypeStruct((b, 6, 8, seq), jnp.float32),
        compiler_params=_params("parallel"), name="mlstm_gate_scans")(gates_t, gate_b.reshape(16, 1))


Q_U, Q_W, Q_WINTER, Q_ENEGM, Q_WK, Q_A = range(6)


def _mlstm_kernel(qf_ref, kf_ref, vf_ref, gf_ref, qb_ref, kb_ref, vb_ref, gb_ref,
                  of_ref, ob_ref, ce_ref, *, chunks):
    @pl.when(pl.program_id(1) == 0)
    def _():
        ce_ref[...] = jnp.zeros(ce_ref.shape, jnp.float32)

    L = ML_CHUNK
    row = lax.broadcasted_iota(jnp.int32, (L, L), 0)
    col = lax.broadcasted_iota(jnp.int32, (L, L), 1)
    lane = lax.broadcasted_iota(jnp.int32, (L, LANES), 1)
    ones_col = jnp.where(lane == 0, 1.0, 0.0).astype(jnp.bfloat16)
    dirs = ((0, qf_ref, kf_ref, vf_ref, gf_ref, of_ref, col <= row, range(chunks)),
            (1, qb_ref, kb_ref, vb_ref, gb_ref, ob_ref, col >= row, range(chunks - 1, -1, -1)))
    for d, q_ref, k_ref, v_ref, g_ref, o_ref, mask, order in dirs:
        n_q, n_r, rows = g_ref.shape[1:]
        cols = g_ref[0].reshape(n_q * n_r, rows).T
        for c in order:
            r0 = c * L
            for h in range(ML_HEADS):
                hs = slice(h * ML_HEAD_DIM, (h + 1) * ML_HEAD_DIM)
                cidx = lambda qn: cols[r0:r0 + L, qn * 8 + d * 4 + h:qn * 8 + d * 4 + h + 1]
                q = q_ref[r0:r0 + L, hs]
                k = k_ref[r0:r0 + L, hs]
                vext = jnp.concatenate([v_ref[r0:r0 + L, hs].astype(jnp.bfloat16), ones_col], axis=1)
                s = lax.dot_general(q, k, (((1,), (1,)), ((), ())), preferred_element_type=jnp.float32)
                w_row = g_ref[0, Q_W, d * 4 + h:d * 4 + h + 1, r0:r0 + L]
                dec = jnp.where(mask, jnp.exp(cidx(Q_U) + w_row), 0.0)
                sd = (s * dec).astype(jnp.bfloat16)
                ce = ce_ref[d * 4 + h]
                inter = jnp.dot(q, ce.astype(jnp.bfloat16), preferred_element_type=jnp.float32)
                intra = jnp.dot(sd, vext, preferred_element_type=jnp.float32)
                tot = cidx(Q_WINTER) * inter + intra
                den = jnp.maximum(jnp.abs(tot[:, ML_HEAD_DIM:ML_HEAD_DIM + 1]), cidx(Q_ENEGM))
                o_ref[r0:r0 + L, hs] = tot[:, :ML_HEAD_DIM] / den
                kw_t = (k.astype(jnp.float32) * cidx(Q_WK)).T.astype(jnp.bfloat16)
                upd = jnp.dot(kw_t, vext, preferred_element_type=jnp.float32)
                a = cols[r0:r0 + 1, Q_A * 8 + d * 4 + h:Q_A * 8 + d * 4 + h + 1]
                ce_ref[d * 4 + h] = a * ce + upd


def _mlstm_scan(qk_conv, proj, gtab, batch, seq, chunks):
    t = qk_conv.shape[0]
    rows = chunks * ML_CHUNK
    ns = seq // rows
    vcb = COL_MLV // ML_WIDTH
    fw = lambda b, j: b * ns + j
    bw = lambda b, j: b * ns + ns - 1 - j
    def specs(rb, lb):
        return [pl.BlockSpec((rows, ML_WIDTH), lambda b, j: (rb(b, j), 0)),
                pl.BlockSpec((rows, ML_WIDTH), lambda b, j: (rb(b, j), 1)),
                pl.BlockSpec((rows, ML_WIDTH), lambda b, j: (rb(b, j), vcb)),
                pl.BlockSpec((1,) + gtab.shape[1:3] + (rows,), lambda b, j: (b, 0, 0, lb(j)))]
    out_f = pl.BlockSpec((rows, ML_WIDTH), lambda b, j: (fw(b, j), 0))
    out_b = pl.BlockSpec((rows, ML_WIDTH), lambda b, j: (bw(b, j), 0))
    kern = functools.partial(_mlstm_kernel, chunks=chunks)
    return pl.pallas_call(
        kern, grid=(batch, ns),
        in_specs=specs(fw, lambda j: j) + specs(bw, lambda j: ns - 1 - j),
        out_specs=[out_f, out_b],
        out_shape=[jax.ShapeDtypeStruct((t, ML_WIDTH), jnp.float32)] * 2,
        scratch_shapes=[pltpu.VMEM((2 * ML_HEADS, ML_HEAD_DIM, 2 * LANES), jnp.float32)],
        compiler_params=_params("parallel", "arbitrary"),
        name="mlstm_scan")(qk_conv, qk_conv, proj, gtab, qk_conv, qk_conv, proj, gtab)


def _layer_norm_rows(z, g, b):
    mu = jnp.mean(z, axis=-1, keepdims=True)
    zc = z - mu
    var = jnp.mean(zc * zc, axis=-1, keepdims=True)
    return zc * lax.rsqrt(var + LN_EPS) * g + b


def _merge_kernel(x_ref, ya_ref, hf_ref, hb_ref, mo_ref, g0_ref, g1_ref, ng_ref,
                  wa_ref, wm_ref, wo_ref, lg_ref, lb_ref, o_ref, ot_ref, otb_ref):
    h = hf_ref[...] + hb_ref[...]
    parts = []
    for hd in range(ML_HEADS):
        hh = h[:, hd * ML_HEAD_DIM:(hd + 1) * ML_HEAD_DIM]
        mu = jnp.mean(hh, axis=-1, keepdims=True)
        hc = hh - mu
        var = jnp.mean(hc * hc, axis=-1, keepdims=True)
        parts.append(hc * lax.rsqrt(var + LN_EPS))
    ym = jnp.concatenate(parts, axis=1) * ng_ref[...] * jax.nn.sigmoid(mo_ref[...])
    pa = jnp.dot(ya_ref[...].astype(jnp.bfloat16), wa_ref[...], preferred_element_type=jnp.float32)
    pm = jnp.dot(ym.astype(jnp.bfloat16), wm_ref[...], preferred_element_type=jnp.float32)
    merged = jax.nn.sigmoid(g0_ref[...]) * pa + jax.nn.sigmoid(g1_ref[...]) * pm
    mix = jnp.dot(merged.astype(jnp.bfloat16), wo_ref[...], preferred_element_type=jnp.float32)
    x1 = _layer_norm_rows(DN_ALPHA * x_ref[...] + mix, lg_ref[...], lb_ref[...])
    o_ref[...] = x1
    x1t = x1.T
    ot_ref[...] = x1t
    otb_ref[...] = x1t.astype(jnp.bfloat16)


def _merge(xt, y_attn, h_fw, h_bw, proj, norm_g, wa, wm, wo, ln_g, ln_b, tm):
    t = xt.shape[0]
    tm = min(tm, t)
    row = lambda w, cb: pl.BlockSpec((tm, w), lambda i: (i, cb))
    full = lambda a: pl.BlockSpec(a.shape, lambda i: (0, 0))
    vec = lambda a: a.reshape(1, -1)
    args = (xt, y_attn, h_fw, h_bw, proj, proj, proj, vec(norm_g), wa, wm, wo, vec(ln_g), vec(ln_b))
    in_specs = [row(D_MODEL, 0), row(DA_WIDTH, 0), row(ML_WIDTH, 0), row(ML_WIDTH, 0),
                row(ML_WIDTH, COL_MLO // ML_WIDTH),
                row(D_MODEL, COL_BR // D_MODEL), row(D_MODEL, COL_BR // D_MODEL + 1)]
    in_specs += [full(a) for a in args[7:]]
    return pl.pallas_call(
        _merge_kernel, grid=(t // tm,), in_specs=in_specs,
        out_specs=[pl.BlockSpec((tm, D_MODEL), lambda i: (i, 0)),
                   pl.BlockSpec((D_MODEL, tm), lambda i: (0, i)),
                   pl.BlockSpec((D_MODEL, tm), lambda i: (0, i))],
        out_shape=[jax.ShapeDtypeStruct((t, D_MODEL), jnp.float32),
                   jax.ShapeDtypeStruct((D_MODEL, t), jnp.float32),
                   jax.ShapeDtypeStruct((D_MODEL, t), jnp.bfloat16)],
        compiler_params=_params("parallel"), name="merge_ln")(*args)


NOT_TOP = float(PEER_TOPK)
N_CAND_ROWS = 80


def _topk_rows(s, dst_ref, want_rank):
    rank = jnp.full(s.shape, NOT_TOP, jnp.float32) if want_rank else None
    for x in range(PEER_TOPK):
        m = jnp.max(s, axis=0, keepdims=True)
        dst_ref[x:x + 1, :] = m
        hit = s == m
        if want_rank:
            rank = jnp.where(hit, float(x), rank)
        s = jnp.where(hit, NEG_INF, s)
    return rank


def _peer_prep_kernel(q_ref, kh_ref, kl_ref, ry_ref, cnt_ref, amp_ref, eb_ref, a_ref, b_ref, sum_ref):
    for head in range(PEER_HEADS):
        _peer_prep_head(head, q_ref, kh_ref, kl_ref, ry_ref, cnt_ref, amp_ref, eb_ref, a_ref, b_ref, sum_ref)


def _peer_prep_head(head, q_ref, kh_ref, kl_ref, ry_ref, cnt_ref, amp_ref, eb_ref, a_ref, b_ref, sum_ref):
    half = N_KEYS
    sc = []
    for p in range(2):
        col = (2 * head + p) * half
        qh, ql = _split_bf16(q_ref[:, col:col + half])
        sc.append(_dot3(kh_ref[p], kl_ref[p], qh, ql, ((1,), (1,))))
    s1, s2 = sc
    _topk_rows(s1, a_ref, False)
    rank2 = _topk_rows(s2, b_ref, True)
    a, b = a_ref[...], b_ref[...]
    h8 = PEER_TOPK // 2
    sum_ref[0:PEER_TOPK, :] = a + b[0:1, :]
    for y in range(1, h8):
        sum_ref[PEER_TOPK + (y - 1) * h8:PEER_TOPK + y * h8, :] = a[0:h8, :] + b[y:y + 1, :]
    sum_ref[N_CAND_ROWS - h8:N_CAND_ROWS, :] = b[h8:PEER_TOPK, :] + a[0:1, :]
    sums = sum_ref[...]
    rest = sums
    for _ in range(PEER_TOPK):
        tau = jnp.max(rest, axis=0, keepdims=True)
        rest = jnp.where(rest == tau, NEG_INF, rest)
    top = a[0:1, :] + b[0:1, :]
    z = jnp.sum(jnp.where(sums >= tau, jnp.exp(sums - top), 0.0), axis=0, keepdims=True)
    cntx = jnp.zeros(a.shape, jnp.float32)
    for y in range(PEER_TOPK):
        cntx = cntx + jnp.where(a + b[y:y + 1, :] >= tau, 1.0, 0.0)
    cnt = jnp.zeros(s1.shape, jnp.float32)
    for x in range(PEER_TOPK):
        cnt = jnp.where(s1 == a[x:x + 1, :], cntx[x:x + 1, :], cnt)
    results = ((ry_ref, rank2.astype(jnp.bfloat16)),
               (cnt_ref, cnt),
               (amp_ref, 0.5 * jnp.exp(s1 - a[0:1, :])),
               (eb_ref, (jnp.exp(s2 - b[0:1, :]) / z).astype(jnp.bfloat16)))
    for ref, val in results:
        for tc in range(ref.shape[1]):
            ref[head, tc] = val[:, tc * LANES:(tc + 1) * LANES]


def _peer_prep(q, keys_hi, keys_lo, tt):
    t = q.shape[0]
    tt = min(tt, t)
    qdim = PEER_HEADS * 2 * N_KEYS
    shape = (PEER_HEADS, t // LANES, N_KEYS, LANES)
    outs = [jax.ShapeDtypeStruct(shape, dt) for dt in (jnp.bfloat16, jnp.float32, jnp.float32, jnp.bfloat16)]
    ospec = pl.BlockSpec((PEER_HEADS, tt // LANES, N_KEYS, LANES), lambda i: (0, i, 0, 0))
    kspec = pl.BlockSpec((2, N_KEYS, N_KEYS), lambda i: (0, 0, 0))
    return pl.pallas_call(
        _peer_prep_kernel, grid=(t // tt,),
        in_specs=[pl.BlockSpec((tt, qdim), lambda i: (i, 0)), kspec, kspec],
        out_specs=[ospec] * 4, out_shape=outs,
        scratch_shapes=[pltpu.VMEM((PEER_TOPK, tt), jnp.float32),
                        pltpu.VMEM((PEER_TOPK, tt), jnp.float32),
                        pltpu.VMEM((N_CAND_ROWS, tt), jnp.float32)],
        compiler_params=_params("parallel"), name="peer_prep")(q, keys_hi, keys_lo)


PEER_KEY_GROUP = 2


def _peer_main_kernel(pre_ref, xf_ref, vt_ref, ry_ref, cnt_ref, amp_ref, eb_ref, lg_ref, lb_ref,
                      o_ref, acc_ref, *, rows_per_step):
    e = pl.program_id(1)
    tb = pre_ref.shape[1]
    bf = jnp.bfloat16

    @pl.when(e == 0)
    def _():
        acc_ref[...] = jnp.zeros(acc_ref.shape, jnp.float32)

    def weigh(ii0):
        zero = jnp.zeros((N_KEYS, LANES), bf)
        out = [[] for _ in range(PEER_KEY_GROUP)]
        for tc in range(tb // LANES):
            cols = slice(tc * LANES, (tc + 1) * LANES)
            spread = lambda ref, h, a: jnp.broadcast_to(
                ref[h, tc, pl.ds(e * rows_per_step + ii0 + a, 1), :], (N_KEYS, LANES)).astype(bf)
            gs = [zero] * PEER_KEY_GROUP
            for h in range(PEER_HEADS):
                ry, eb = ry_ref[h, tc], eb_ref[h, tc]
                for a in range(PEER_KEY_GROUP):
                    sel = jnp.maximum(jnp.minimum(spread(cnt_ref, h, a) - ry, eb), zero)
                    gs[a] = gs[a] + spread(amp_ref, h, a) * sel
            for a in range(PEER_KEY_GROUP):
                rows = slice((ii0 + a) * N_KEYS, (ii0 + a + 1) * N_KEYS)
                pre = pre_ref[rows, cols].astype(jnp.float32)
                act = pre * (1.0 + lax.erf(pre * (2.0 ** -0.5)))
                out[a].append(gs[a] * act.astype(bf))
        return jnp.concatenate([jnp.concatenate(tiles, axis=1) for tiles in out], axis=0)

    wgt = jnp.concatenate([weigh(ii0) for ii0 in range(0, rows_per_step, PEER_KEY_GROUP)], axis=0)
    acc_ref[...] += jnp.dot(vt_ref[...], wgt, preferred_element_type=jnp.float32)

    @pl.when(e == pl.num_programs(1) - 1)
    def _():
        z = DN_ALPHA * xf_ref[...] + acc_ref[...]
        mu = jnp.mean(z, axis=0, keepdims=True)
        zc = z - mu
        var = jnp.mean(zc * zc, axis=0, keepdims=True)
        o_ref[...] = (zc * lax.rsqrt(var + LN_EPS)).T * lg_ref[...] + lb_ref[...]


def _peer_main(pre, x1t, vt_bf, ry, cnt, amp, eb, ln_g, ln_b, tb, eb_rows):
    d, t = x1t.shape
    tb = min(tb, t)
    rows_per_step = eb_rows // N_KEYS
    tab = lambda: pl.BlockSpec((PEER_HEADS, tb // LANES, N_KEYS, LANES), lambda i, e: (0, i, 0, 0))
    colv = lambda: pl.BlockSpec((1, d), lambda i, e: (0, 0))
    kern = functools.partial(_peer_main_kernel, rows_per_step=rows_per_step)
    return pl.pallas_call(
        kern, grid=(t // tb, N_EXPERTS // eb_rows),
        in_specs=[pl.BlockSpec((eb_rows, tb), lambda i, e: (e, i)),
                  pl.BlockSpec((d, tb), lambda i, e: (0, i)),
                  pl.BlockSpec((d, eb_rows), lambda i, e: (0, e)),
                  tab(), tab(), tab(), tab(), colv(), colv()],
        out_specs=pl.BlockSpec((tb, d), lambda i, e: (i, 0)),
        out_shape=jax.ShapeDtypeStruct((t, d), jnp.float32),
        scratch_shapes=[pltpu.VMEM((d, tb), jnp.float32)],
        compiler_params=_params("parallel", "arbitrary"),
        name="peer_dense")(pre, x1t, vt_bf, ry, cnt, amp, eb, ln_g.reshape(1, d), ln_b.reshape(1, d))


def _token_mixing(xt, batch, seq, w_in, da_lambda, da_subln_g, ml_conv_w, ml_conv_b, ml_gate_b,
                  ml_norm_g, w_branch_attn, w_branch_mlstm, w_out, ln1_g, ln1_b):
    bf = jnp.bfloat16
    zpad = jnp.zeros((D_MODEL, LANES - N_GATE_COLS), jnp.float32)
    w_cols = jnp.concatenate([w_in[:, IN_MAIN + N_GATE_COLS:], w_in[:, :IN_MAIN],
                              w_in[:, IN_MAIN:IN_MAIN + N_GATE_COLS], zpad], axis=1).astype(bf)
    proj = _matmul(xt, w_cols, *TILE_PROJ, "in_projection")

    qkv = _rope_cast(proj, seq, TILE_ROWS)
    y_attn = _diff_attention(qkv, da_lambda, da_subln_g, batch, seq, *TILE_ATTN)

    qk_conv = _conv_silu(proj, ml_conv_w, ml_conv_b, seq, TILE_ROWS)
    order = jnp.array([0, 2, 1, 3])
    gates_t = proj[:, COL_MLG:COL_MLG + N_GATE_COLS].reshape(batch, seq, 4, ML_HEADS)[:, :, order]
    gates_t = gates_t.reshape(batch, seq, N_GATE_COLS).transpose(0, 2, 1)
    gtab = _gate_scans(gates_t, ml_gate_b[order].reshape(N_GATE_COLS))
    h_fw, h_bw = _mlstm_scan(qk_conv, proj, gtab, batch, seq, min(MLSTM_CHUNKS, seq // ML_CHUNK))

    return _merge(xt, y_attn, h_fw, h_bw, proj, ml_norm_g,
                  w_branch_attn.astype(bf), w_branch_mlstm.astype(bf), w_out.astype(bf),
                  ln1_g, ln1_b, TILE_MERGE)


def _peer_layer(x1, x1t, x1t_bf, peer_w_q, peer_sub_keys, peer_u, peer_v, ln2_g, ln2_b):
    bf = jnp.bfloat16
    wq_hi, wq_lo = _split_bf16(peer_w_q)
    q = _matmul3(x1, wq_hi, wq_lo, *TILE_QPROJ)
    k_hi, k_lo = _split_bf16(peer_sub_keys)
    ry, cnt, amp, eb = _peer_prep(q, k_hi, k_lo, TILE_PREP)
    pre = _matmul(peer_u.astype(bf), x1t_bf, *TILE_PREACT, "peer_preact", bf)
    return _peer_main(pre, x1t, peer_v.T.astype(bf), ry, cnt, amp, eb, ln2_g, ln2_b, *TILE_PEER)


def kernel(x, w_in, da_lambda, da_subln_g, ml_conv_w, ml_conv_b, ml_gate_b, ml_norm_g,
           w_branch_attn, w_branch_mlstm, w_out, ln1_g, ln1_b,
           peer_w_q, peer_sub_keys, peer_u, peer_v, ln2_g, ln2_b):
    batch, seq, d = x.shape
    xt = x.reshape(batch * seq, d)
    x1, x1t, x1t_bf = _token_mixing(xt, batch, seq, w_in[0], da_lambda[0], da_subln_g[0], ml_conv_w[0],
                                    ml_conv_b[0], ml_gate_b[0], ml_norm_g[0], w_branch_attn[0],
                                    w_branch_mlstm[0], w_out[0], ln1_g[0], ln1_b[0])
    out = _peer_layer(x1, x1t, x1t_bf, peer_w_q[0], peer_sub_keys[0], peer_u[0], peer_v[0],
                      ln2_g[0], ln2_b[0])
    return out.reshape(batch, seq, d)
```

```python
import functools
import math

import jax
import jax.numpy as jnp
from jax import lax
from jax.experimental import pallas as pl
from jax.experimental.pallas import tpu as pltpu

D_MODEL = 1024
DA_QK_DIM = 64
DA_V_DIM = 128
DA_HEADS = 4
DA_WIDTH = 512
ROPE_DIM = 16
ROPE_THETA = 500000.0
ML_WIDTH = 512
ML_HEADS = 4
ML_HEAD_DIM = 128
ML_CONV = 5
ML_CHUNK = 64
STAB_INIT = -1e30
N_KEYS = 128
N_EXPERTS = N_KEYS * N_KEYS
PEER_HEADS = 8
PEER_TOPK = 16
DEPTH = 1
DN_ALPHA = (2 * DEPTH) ** 0.25
LN_EPS = 1e-5
LAMBDA_INIT = 0.8 - 0.6 * math.exp(-0.3 * 0)

LANES = 128
NEG_INF = float("-inf")
VMEM_LIMIT = 48 * 1024 * 1024

TILE_PROJ = (1024, 1920)
TILE_ROWS = 2048
TILE_ATTN = (256, 4096)
MLSTM_CHUNKS = 4
TILE_MERGE = 512
TILE_QPROJ = (1024, 1024)
TILE_PREP = 256
TILE_PREACT = (1024, 1024)
TILE_PEER = (512, 2048)

COL_BR = 0
COL_Q, COL_K, COL_V = 2048, 2560, 3072
COL_MLQ, COL_MLK, COL_MLV, COL_MLO = 3584, 4096, 4608, 5120
COL_MLG = 5632
N_GATE_COLS = 4 * ML_HEADS
IN_MAIN = 3584


def _params(*sem):
    return pltpu.CompilerParams(dimension_semantics=sem, vmem_limit_bytes=VMEM_LIMIT)


def _mm_kernel(x_ref, w_ref, o_ref):
    o_ref[...] = jnp.dot(x_ref[...].astype(jnp.bfloat16), w_ref[...],
                         preferred_element_type=jnp.float32).astype(o_ref.dtype)


def _matmul(x, w, tm, tn, name, out_dtype=jnp.float32):
    m, k = x.shape
    n = w.shape[1]
    tm, tn = min(tm, m), min(tn, n)
    return pl.pallas_call(
        _mm_kernel, grid=(m // tm, n // tn),
        in_specs=[pl.BlockSpec((tm, k), lambda i, j: (i, 0)),
                  pl.BlockSpec((k, tn), lambda i, j: (0, j))],
        out_specs=pl.BlockSpec((tm, tn), lambda i, j: (i, j)),
        out_shape=jax.ShapeDtypeStruct((m, n), out_dtype),
        compiler_params=_params("parallel", "parallel"), name=name)(x, w)


def _split_bf16(a):
    hi = a.astype(jnp.bfloat16)
    lo = (a - hi.astype(jnp.float32)).astype(jnp.bfloat16)
    return hi, lo


def _dot3(a_hi, a_lo, b_hi, b_lo, dims):
    f = lambda a, b: lax.dot_general(a, b, (dims, ((), ())), preferred_element_type=jnp.float32)
    return f(a_hi, b_hi) + f(a_hi, b_lo) + f(a_lo, b_hi)


def _mm3_kernel(x_ref, wh_ref, wl_ref, o_ref):
    xh, xl = _split_bf16(x_ref[...])
    o_ref[...] = _dot3(xh, xl, wh_ref[...], wl_ref[...], ((1,), (0,)))


def _matmul3(x, w_hi, w_lo, tm, tn):
    m, k = x.shape
    n = w_hi.shape[1]
    tm, tn = min(tm, m), min(tn, n)
    return pl.pallas_call(
        _mm3_kernel, grid=(m // tm, n // tn),
        in_specs=[pl.BlockSpec((tm, k), lambda i, j: (i, 0)),
                  pl.BlockSpec((k, tn), lambda i, j: (0, j)),
                  pl.BlockSpec((k, tn), lambda i, j: (0, j))],
        out_specs=pl.BlockSpec((tm, tn), lambda i, j: (i, j)),
        out_shape=jax.ShapeDtypeStruct((m, n), jnp.float32),
        compiler_params=_params("parallel", "parallel"), name="dense_matmul3")(x, w_hi, w_lo)


def _rope_kernel(q_ref, k_ref, v_ref, c_ref, s1_ref, s2_ref, o_ref):
    c, s1, s2 = c_ref[...], s1_ref[...], s2_ref[...]
    for which, src in enumerate((q_ref, k_ref)):
        for hd in range(DA_HEADS):
            x = src[:, hd * LANES:(hd + 1) * LANES]
            r = x * c + pltpu.roll(x, LANES - 8, 1) * s1 + pltpu.roll(x, 8, 1) * s2
            if which == 0:
                r = r * (DA_QK_DIM ** -0.5)
            col = which * DA_WIDTH + hd * LANES
            o_ref[:, col:col + LANES] = r.astype(jnp.bfloat16)
    o_ref[:, 2 * DA_WIDTH:3 * DA_WIDTH] = v_ref[...].astype(jnp.bfloat16)


def _rope_tables(seq):
    pos = jnp.arange(seq, dtype=jnp.float32)
    inv_freq = ROPE_THETA ** (-jnp.arange(0, ROPE_DIM, 2, dtype=jnp.float32) / ROPE_DIM)
    ang = pos[:, None] * inv_freq[None, :]
    cos, sin = jnp.cos(ang), jnp.sin(ang)
    half = ROPE_DIM // 2
    zeros = jnp.zeros((seq, DA_QK_DIM - ROPE_DIM), jnp.float32)
    z8 = jnp.zeros((seq, half), jnp.float32)
    c64 = jnp.concatenate([cos, cos, zeros + 1.0], -1)
    s1_64 = jnp.concatenate([-sin, z8, zeros], -1)
    s2_64 = jnp.concatenate([z8, sin, zeros], -1)
    tile2 = lambda t: jnp.concatenate([t, t], -1)
    return tile2(c64), tile2(s1_64), tile2(s2_64)


def _rope_cast(proj, seq, tm):
    t = proj.shape[0]
    tm = min(tm, seq)
    nsb = seq // tm
    c, s1, s2 = _rope_tables(seq)
    tab = pl.BlockSpec((tm, LANES), lambda i: (i % nsb, 0))
    src = lambda col: pl.BlockSpec((tm, DA_WIDTH), lambda i: (i, col // DA_WIDTH))
    return pl.pallas_call(
        _rope_kernel, grid=(t // tm,),
        in_specs=[src(COL_Q), src(COL_K), src(COL_V), tab, tab, tab],
        out_specs=pl.BlockSpec((tm, 3 * DA_WIDTH), lambda i: (i, 0)),
        out_shape=jax.ShapeDtypeStruct((t, 3 * DA_WIDTH), jnp.bfloat16),
        compiler_params=_params("parallel"), name="rope_cast")(proj, proj, proj, c, s1, s2)


def _attn_kernel(lam_ref, g_ref, q_ref, k_ref, v_ref, o_ref, m_ref, acc_ref, sa_ref, sb_ref, *, tk):
    nk = k_ref.shape[0] // tk
    m_ref[...] = jnp.full(m_ref.shape, NEG_INF, jnp.float32)
    acc_ref[...] = jnp.zeros(acc_ref.shape, jnp.float32)
    lane = lax.broadcasted_iota(jnp.int32, (tk, LANES), 1)
    ones_col = jnp.where(lane == 0, 1.0, 0.0).astype(jnp.bfloat16)

    def scores(j, s_ref):
        k = k_ref[pl.ds(pl.multiple_of(j * tk, tk), tk), :]
        for mp in range(2):
            q = q_ref[:, mp * DA_QK_DIM:(mp + 1) * DA_QK_DIM]
            s_ref[mp] = lax.dot_general(q, k[:, mp * DA_QK_DIM:(mp + 1) * DA_QK_DIM],
                                        (((1,), (1,)), ((), ())), preferred_element_type=jnp.float32)

    def accumulate(j, s_ref):
        vext = jnp.concatenate([v_ref[pl.ds(pl.multiple_of(j * tk, tk), tk), :], ones_col], axis=1)
        for mp in range(2):
            s = s_ref[mp]
            m_old = m_ref[mp]
            m_new = jnp.maximum(m_old, jnp.max(s, axis=-1, keepdims=True))
            p = jnp.exp(s - m_new).astype(jnp.bfloat16)
            pv = jnp.dot(p, vext, preferred_element_type=jnp.float32)
            acc_ref[mp] = jnp.exp(m_old - m_new) * acc_ref[mp] + pv
            m_ref[mp] = m_new

    scores(0, sa_ref)

    def body(jj, carry):
        j = 2 * jj
        scores(j + 1, sb_ref)
        accumulate(j, sa_ref)
        scores(j + 2, sa_ref)
        accumulate(j + 1, sb_ref)
        return carry

    lax.fori_loop(0, nk // 2 - 1, body, 0)
    scores(nk - 1, sb_ref)
    accumulate(nk - 2, sa_ref)
    accumulate(nk - 1, sb_ref)

    lp = lam_ref[...]
    lam = (jnp.exp(jnp.sum(lp[0:1] * lp[1:2], axis=-1, keepdims=True))
           - jnp.exp(jnp.sum(lp[2:3] * lp[3:4], axis=-1, keepdims=True)) + LAMBDA_INIT)
    a0, a1 = acc_ref[0], acc_ref[1]
    o = (a0[:, :DA_V_DIM] / a0[:, DA_V_DIM:DA_V_DIM + 1]
         - lam * (a1[:, :DA_V_DIM] / a1[:, DA_V_DIM:DA_V_DIM + 1]))
    y = o * lax.rsqrt(jnp.mean(o * o, axis=-1, keepdims=True) + LN_EPS)
    o_ref[...] = y * g_ref[...] * (1.0 - LAMBDA_INIT)


def _diff_attention(qkv, da_lambda, subln_g, batch, seq, tq, tk):
    t = qkv.shape[0]
    tq, tk = min(tq, seq), min(tk, seq // 2)
    assert (seq // tk) % 2 == 0, "the key loop is software-pipelined over chunk pairs"
    nq = seq // tq
    kern = functools.partial(_attn_kernel, tk=tk)
    return pl.pallas_call(
        kern, grid=(batch, DA_HEADS, nq),
        in_specs=[pl.BlockSpec((4, DA_QK_DIM), lambda b, h, i: (0, 0)),
                  pl.BlockSpec((1, DA_V_DIM), lambda b, h, i: (0, 0)),
                  pl.BlockSpec((tq, LANES), lambda b, h, i: (b * nq + i, h)),
                  pl.BlockSpec((seq, LANES), lambda b, h, i: (b, DA_HEADS + h)),
                  pl.BlockSpec((seq, LANES), lambda b, h, i: (b, 2 * DA_HEADS + h))],
        out_specs=pl.BlockSpec((tq, LANES), lambda b, h, i: (b * nq + i, h)),
        out_shape=jax.ShapeDtypeStruct((t, DA_WIDTH), jnp.float32),
        scratch_shapes=[pltpu.VMEM((2, tq, 1), jnp.float32),
                        pltpu.VMEM((2, tq, 2 * LANES), jnp.float32),
                        pltpu.VMEM((2, tq, tk), jnp.float32),
                        pltpu.VMEM((2, tq, tk), jnp.float32)],
        compiler_params=_params("parallel", "parallel", "parallel"),
        name="diff_attention")(da_lambda, subln_g.reshape(1, DA_V_DIM), qkv, qkv, qkv)


def _conv_kernel(prev_ref, cur_ref, next_ref, w_ref, b_ref, o_ref, *, nsb):
    i, c = pl.program_id(0), pl.program_id(1)
    tm = cur_ref.shape[0]
    sb = i % nsb
    prev = jnp.where(sb == 0, 0.0, prev_ref[...])
    nxt = jnp.where(sb == nsb - 1, 0.0, next_ref[...])
    ext = jnp.concatenate([prev, cur_ref[...], nxt], axis=0)
    n = tm + 16
    pad = (ML_CONV - 1) // 2
    acc = jnp.zeros((tm, cur_ref.shape[1]), jnp.float32) + b_ref[...]
    for w in range(ML_CONV):
        shift = (pad - w) % n
        z = ext if shift == 0 else pltpu.roll(ext, shift, 0)
        acc = acc + z[8:8 + tm] * w_ref[w:w + 1, :]
    y = acc * jax.nn.sigmoid(acc)
    y = y * jnp.where(c == 1, ML_HEAD_DIM ** -0.5, 1.0)
    o_ref[...] = y.astype(jnp.bfloat16)


def _conv_silu(proj, conv_w, conv_b, seq, tm):
    t = proj.shape[0]
    tm = min(tm, seq)
    nsb = seq // tm
    r8 = tm // 8
    nrow8 = t // 8
    cb0 = COL_MLQ // ML_WIDTH
    kern = functools.partial(_conv_kernel, nsb=nsb)
    return pl.pallas_call(
        kern, grid=(t // tm, 2),
        in_specs=[pl.BlockSpec((8, ML_WIDTH), lambda i, c: (jnp.maximum(i * r8 - 1, 0), cb0 + c)),
                  pl.BlockSpec((tm, ML_WIDTH), lambda i, c: (i, cb0 + c)),
                  pl.BlockSpec((8, ML_WIDTH), lambda i, c: (jnp.minimum((i + 1) * r8, nrow8 - 1), cb0 + c)),
                  pl.BlockSpec((ML_CONV, ML_WIDTH), lambda i, c: (0, c)),
                  pl.BlockSpec((1, ML_WIDTH), lambda i, c: (0, c))],
        out_specs=pl.BlockSpec((tm, ML_WIDTH), lambda i, c: (i, c)),
        out_shape=jax.ShapeDtypeStruct((t, 2 * ML_WIDTH), jnp.bfloat16),
        compiler_params=_params("parallel", "parallel"),
        name="conv_silu")(proj, proj, proj, conv_w, conv_b.reshape(1, 2 * ML_WIDTH))


def _seg_scan(x, pos, seq, op, fill, reverse):
    y = x
    d = 1
    while d < ML_CHUNK:
        if reverse:
            y = op(y, jnp.where(pos < ML_CHUNK - d, pltpu.roll(y, seq - d, 1), fill))
        else:
            y = op(y, jnp.where(pos >= d, pltpu.roll(y, d, 1), fill))
        d *= 2
    return y


def _gate_dir(i_pre, logf, pos, lane, seq, reverse):
    bj = _seg_scan(logf, pos, seq, jnp.add, 0.0, reverse)
    last = (pos == 0) if reverse else (pos == ML_CHUNK - 1)
    b_tot = _seg_scan(jnp.where(last, bj, NEG_INF), pos, seq, jnp.maximum, NEG_INF, not reverse)
    w = i_pre - bj
    cmax_w = _seg_scan(w, pos, seq, jnp.maximum, NEG_INF, reverse)
    g = b_tot - bj + i_pre
    g_max = jnp.maximum(_seg_scan(g, pos, seq, jnp.maximum, NEG_INF, False),
                        _seg_scan(g, pos, seq, jnp.maximum, NEG_INF, True))
    acc_a, acc_g = b_tot, g_max
    d = ML_CHUNK
    while d < seq:
        if reverse:
            ok = lane < seq - d
            pa = jnp.where(ok, pltpu.roll(acc_a, seq - d, 1), 0.0)
            pg = jnp.where(ok, pltpu.roll(acc_g, seq - d, 1), NEG_INF)
        else:
            ok = lane >= d
            pa = jnp.where(ok, pltpu.roll(acc_a, d, 1), 0.0)
            pg = jnp.where(ok, pltpu.roll(acc_g, d, 1), NEG_INF)
        acc_g = jnp.maximum(pg + acc_a, acc_g)
        acc_a = pa + acc_a
        d *= 2
    m_after = jnp.maximum(acc_a + STAB_INIT, acc_g)
    if reverse:
        m_before = jnp.where(lane < seq - ML_CHUNK, pltpu.roll(m_after, seq - ML_CHUNK, 1), STAB_INIT)
    else:
        m_before = jnp.where(lane >= ML_CHUNK, pltpu.roll(m_after, ML_CHUNK, 1), STAB_INIT)
    mr_rel = jnp.maximum(m_before, cmax_w)
    m_row = bj + mr_rel
    u = -mr_rel
    w_inter = jnp.exp(m_before - mr_rel)
    e_negm = jnp.exp(-m_row)
    wk = jnp.exp(g - m_after)
    a = jnp.exp(b_tot + m_before - m_after)
    return u, w, w_inter, e_negm, wk, a


def _gate_kernel(g_ref, b_ref, o_ref):
    seq = g_ref.shape[2]
    g = g_ref[0] + b_ref[...]
    i_pre, f_pre = g[0:8], g[8:16]
    logf = jnp.minimum(f_pre, 0.0) - jnp.log1p(jnp.exp(-jnp.abs(f_pre)))
    lane = lax.broadcasted_iota(jnp.int32, (2 * ML_HEADS, seq), 1)
    row = lax.broadcasted_iota(jnp.int32, (2 * ML_HEADS, seq), 0)
    pos = lane % ML_CHUNK
    fwd = _gate_dir(i_pre, logf, pos, lane, seq, False)
    bwd = _gate_dir(i_pre, logf, pos, lane, seq, True)
    for q, (af, ab) in enumerate(zip(fwd, bwd)):
        o_ref[0, q] = jnp.where(row < ML_HEADS, af, ab)


def _gate_scans(gates_t, gate_b):
    b, _, seq = gates_t.shape
    return pl.pallas_call(
        _gate_kernel, grid=(b,),
        in_specs=[pl.BlockSpec((1, 16, seq), lambda i: (i, 0, 0)),
                  pl.BlockSpec((16, 1), lambda i: (0, 0))],
        out_specs=pl.BlockSpec((1, 6, 8, seq), lambda i: (i, 0, 0, 0)),
        out_shape=jax.ShapeDtypeStruct((b, 6, 8, seq), jnp.float32),
        compiler_params=_params("parallel"), name="mlstm_gate_scans")(gates_t, gate_b.reshape(16, 1))


Q_U, Q_W, Q_WINTER, Q_ENEGM, Q_WK, Q_A = range(6)


def _mlstm_kernel(qf_ref, kf_ref, vf_ref, gf_ref, qb_ref, kb_ref, vb_ref, gb_ref,
                  of_ref, ob_ref, ce_ref, *, chunks):
    @pl.when(pl.program_id(1) == 0)
    def _():
        ce_ref[...] = jnp.zeros(ce_ref.shape, jnp.float32)

    L = ML_CHUNK
    row = lax.broadcasted_iota(jnp.int32, (L, L), 0)
    col = lax.broadcasted_iota(jnp.int32, (L, L), 1)
    lane = lax.broadcasted_iota(jnp.int32, (L, LANES), 1)
    ones_col = jnp.where(lane == 0, 1.0, 0.0).astype(jnp.bfloat16)
    dirs = ((0, qf_ref, kf_ref, vf_ref, gf_ref, of_ref, col <= row, range(chunks)),
            (1, qb_ref, kb_ref, vb_ref, gb_ref, ob_ref, col >= row, range(chunks - 1, -1, -1)))
    for d, q_ref, k_ref, v_ref, g_ref, o_ref, mask, order in dirs:
        n_q, n_r, rows = g_ref.shape[1:]
        cols = g_ref[0].reshape(n_q * n_r, rows).T
        for c in order:
            r0 = c * L
            for h in range(ML_HEADS):
                hs = slice(h * ML_HEAD_DIM, (h + 1) * ML_HEAD_DIM)
                cidx = lambda qn: cols[r0:r0 + L, qn * 8 + d * 4 + h:qn * 8 + d * 4 + h + 1]
                q = q_ref[r0:r0 + L, hs]
                k = k_ref[r0:r0 + L, hs]
                vext = jnp.concatenate([v_ref[r0:r0 + L, hs].astype(jnp.bfloat16), ones_col], axis=1)
                s = lax.dot_general(q, k, (((1,), (1,)), ((), ())), preferred_element_type=jnp.float32)
                w_row = g_ref[0, Q_W, d * 4 + h:d * 4 + h + 1, r0:r0 + L]
                dec = jnp.where(mask, jnp.exp(cidx(Q_U) + w_row), 0.0)
                sd = (s * dec).astype(jnp.bfloat16)
                ce = ce_ref[d * 4 + h]
                inter = jnp.dot(q, ce.astype(jnp.bfloat16), preferred_element_type=jnp.float32)
                intra = jnp.dot(sd, vext, preferred_element_type=jnp.float32)
                tot = cidx(Q_WINTER) * inter + intra
                den = jnp.maximum(jnp.abs(tot[:, ML_HEAD_DIM:ML_HEAD_DIM + 1]), cidx(Q_ENEGM))
                o_ref[r0:r0 + L, hs] = tot[:, :ML_HEAD_DIM] / den
                kw_t = (k.astype(jnp.float32) * cidx(Q_WK)).T.astype(jnp.bfloat16)
                upd = jnp.dot(kw_t, vext, preferred_element_type=jnp.float32)
                a = cols[r0:r0 + 1, Q_A * 8 + d * 4 + h:Q_A * 8 + d * 4 + h + 1]
                ce_ref[d * 4 + h] = a * ce + upd


def _mlstm_scan(qk_conv, proj, gtab, batch, seq, chunks):
    t = qk_conv.shape[0]
    rows = chunks * ML_CHUNK
    ns = seq // rows
    vcb = COL_MLV // ML_WIDTH
    fw = lambda b, j: b * ns + j
    bw = lambda b, j: b * ns + ns - 1 - j
    def specs(rb, lb):
        return [pl.BlockSpec((rows, ML_WIDTH), lambda b, j: (rb(b, j), 0)),
                pl.BlockSpec((rows, ML_WIDTH), lambda b, j: (rb(b, j), 1)),
                pl.BlockSpec((rows, ML_WIDTH), lambda b, j: (rb(b, j), vcb)),
                pl.BlockSpec((1,) + gtab.shape[1:3] + (rows,), lambda b, j: (b, 0, 0, lb(j)))]
    out_f = pl.BlockSpec((rows, ML_WIDTH), lambda b, j: (fw(b, j), 0))
    out_b = pl.BlockSpec((rows, ML_WIDTH), lambda b, j: (bw(b, j), 0))
    kern = functools.partial(_mlstm_kernel, chunks=chunks)
    return pl.pallas_call(
        kern, grid=(batch, ns),
        in_specs=specs(fw, lambda j: j) + specs(bw, lambda j: ns - 1 - j),
        out_specs=[out_f, out_b],
        out_shape=[jax.ShapeDtypeStruct((t, ML_WIDTH), jnp.float32)] * 2,
        scratch_shapes=[pltpu.VMEM((2 * ML_HEADS, ML_HEAD_DIM, 2 * LANES), jnp.float32)],
        compiler_params=_params("parallel", "arbitrary"),
        name="mlstm_scan")(qk_conv, qk_conv, proj, gtab, qk_conv, qk_conv, proj, gtab)


def _layer_norm_rows(z, g, b):
    mu = jnp.mean(z, axis=-1, keepdims=True)
    zc = z - mu
    var = jnp.mean(zc * zc, axis=-1, keepdims=True)
    return zc * lax.rsqrt(var + LN_EPS) * g + b


def _merge_kernel(x_ref, ya_ref, hf_ref, hb_ref, mo_ref, g0_ref, g1_ref, ng_ref,
                  wa_ref, wm_ref, wo_ref, lg_ref, lb_ref, o_ref, ot_ref, otb_ref):
    h = hf_ref[...] + hb_ref[...]
    parts = []
    for hd in range(ML_HEADS):
        hh = h[:, hd * ML_HEAD_DIM:(hd + 1) * ML_HEAD_DIM]
        mu = jnp.mean(hh, axis=-1, keepdims=True)
        hc = hh - mu
        var = jnp.mean(hc * hc, axis=-1, keepdims=True)
        parts.append(hc * lax.rsqrt(var + LN_EPS))
    ym = jnp.concatenate(parts, axis=1) * ng_ref[...] * jax.nn.sigmoid(mo_ref[...])
    pa = jnp.dot(ya_ref[...].astype(jnp.bfloat16), wa_ref[...], preferred_element_type=jnp.float32)
    pm = jnp.dot(ym.astype(jnp.bfloat16), wm_ref[...], preferred_element_type=jnp.float32)
    merged = jax.nn.sigmoid(g0_ref[...]) * pa + jax.nn.sigmoid(g1_ref[...]) * pm
    mix = jnp.dot(merged.astype(jnp.bfloat16), wo_ref[...], preferred_element_type=jnp.float32)
    x1 = _layer_norm_rows(DN_ALPHA * x_ref[...] + mix, lg_ref[...], lb_ref[...])
    o_ref[...] = x1
    x1t = x1.T
    ot_ref[...] = x1t
    otb_ref[...] = x1t.astype(jnp.bfloat16)


def _merge(xt, y_attn, h_fw, h_bw, proj, norm_g, wa, wm, wo, ln_g, ln_b, tm):
    t = xt.shape[0]
    tm = min(tm, t)
    row = lambda w, cb: pl.BlockSpec((tm, w), lambda i: (i, cb))
    full = lambda a: pl.BlockSpec(a.shape, lambda i: (0, 0))
    vec = lambda a: a.reshape(1, -1)
    args = (xt, y_attn, h_fw, h_bw, proj, proj, proj, vec(norm_g), wa, wm, wo, vec(ln_g), vec(ln_b))
    in_specs = [row(D_MODEL, 0), row(DA_WIDTH, 0), row(ML_WIDTH, 0), row(ML_WIDTH, 0),
                row(ML_WIDTH, COL_MLO // ML_WIDTH),
                row(D_MODEL, COL_BR // D_MODEL), row(D_MODEL, COL_BR // D_MODEL + 1)]
    in_specs += [full(a) for a in args[7:]]
    return pl.pallas_call(
        _merge_kernel, grid=(t // tm,), in_specs=in_specs,
        out_specs=[pl.BlockSpec((tm, D_MODEL), lambda i: (i, 0)),
                   pl.BlockSpec((D_MODEL, tm), lambda i: (0, i)),
                   pl.BlockSpec((D_MODEL, tm), lambda i: (0, i))],
        out_shape=[jax.ShapeDtypeStruct((t, D_MODEL), jnp.float32),
                   jax.ShapeDtypeStruct((D_MODEL, t), jnp.float32),
                   jax.ShapeDtypeStruct((D_MODEL, t), jnp.bfloat16)],
        compiler_params=_params("parallel"), name="merge_ln")(*args)


NOT_TOP = float(PEER_TOPK)
N_CAND_ROWS = 80


def _topk_rows(s, dst_ref, want_rank):
    rank = jnp.full(s.shape, NOT_TOP, jnp.float32) if want_rank else None
    for x in range(PEER_TOPK):
        m = jnp.max(s, axis=0, keepdims=True)
        dst_ref[x:x + 1, :] = m
        hit = s == m
        if want_rank:
            rank = jnp.where(hit, float(x), rank)
        s = jnp.where(hit, NEG_INF, s)
    return rank


def _peer_prep_kernel(q_ref, kh_ref, kl_ref, ry_ref, cnt_ref, amp_ref, eb_ref, a_ref, b_ref, sum_ref):
    for head in range(PEER_HEADS):
        _peer_prep_head(head, q_ref, kh_ref, kl_ref, ry_ref, cnt_ref, amp_ref, eb_ref, a_ref, b_ref, sum_ref)


def _peer_prep_head(head, q_ref, kh_ref, kl_ref, ry_ref, cnt_ref, amp_ref, eb_ref, a_ref, b_ref, sum_ref):
    half = N_KEYS
    sc = []
    for p in range(2):
        col = (2 * head + p) * half
        qh, ql = _split_bf16(q_ref[:, col:col + half])
        sc.append(_dot3(kh_ref[p], kl_ref[p], qh, ql, ((1,), (1,))))
    s1, s2 = sc
    _topk_rows(s1, a_ref, False)
    rank2 = _topk_rows(s2, b_ref, True)
    a, b = a_ref[...], b_ref[...]
    h8 = PEER_TOPK // 2
    sum_ref[0:PEER_TOPK, :] = a + b[0:1, :]
    for y in range(1, h8):
        sum_ref[PEER_TOPK + (y - 1) * h8:PEER_TOPK + y * h8, :] = a[0:h8, :] + b[y:y + 1, :]
    sum_ref[N_CAND_ROWS - h8:N_CAND_ROWS, :] = b[h8:PEER_TOPK, :] + a[0:1, :]
    sums = sum_ref[...]
    rest = sums
    for _ in range(PEER_TOPK):
        tau = jnp.max(rest, axis=0, keepdims=True)
        rest = jnp.where(rest == tau, NEG_INF, rest)
    top = a[0:1, :] + b[0:1, :]
    z = jnp.sum(jnp.where(sums >= tau, jnp.exp(sums - top), 0.0), axis=0, keepdims=True)
    cntx = jnp.zeros(a.shape, jnp.float32)
    for y in range(PEER_TOPK):
        cntx = cntx + jnp.where(a + b[y:y + 1, :] >= tau, 1.0, 0.0)
    cnt = jnp.zeros(s1.shape, jnp.float32)
    for x in range(PEER_TOPK):
        cnt = jnp.where(s1 == a[x:x + 1, :], cntx[x:x + 1, :], cnt)
    results = ((ry_ref, rank2.astype(jnp.bfloat16)),
               (cnt_ref, cnt),
               (amp_ref, 0.5 * jnp.exp(s1 - a[0:1, :])),
               (eb_ref, (jnp.exp(s2 - b[0:1, :]) / z).astype(jnp.bfloat16)))
    for ref, val in results:
        for tc in range(ref.shape[1]):
            ref[head, tc] = val[:, tc * LANES:(tc + 1) * LANES]


def _peer_prep(q, keys_hi, keys_lo, tt):
    t = q.shape[0]
    tt = min(tt, t)
    qdim = PEER_HEADS * 2 * N_KEYS
    shape = (PEER_HEADS, t // LANES, N_KEYS, LANES)
    outs = [jax.ShapeDtypeStruct(shape, dt) for dt in (jnp.bfloat16, jnp.float32, jnp.float32, jnp.bfloat16)]
    ospec = pl.BlockSpec((PEER_HEADS, tt // LANES, N_KEYS, LANES), lambda i: (0, i, 0, 0))
    kspec = pl.BlockSpec((2, N_KEYS, N_KEYS), lambda i: (0, 0, 0))
    return pl.pallas_call(
        _peer_prep_kernel, grid=(t // tt,),
        in_specs=[pl.BlockSpec((tt, qdim), lambda i: (i, 0)), kspec, kspec],
        out_specs=[ospec] * 4, out_shape=outs,
        scratch_shapes=[pltpu.VMEM((PEER_TOPK, tt), jnp.float32),
                        pltpu.VMEM((PEER_TOPK, tt), jnp.float32),
                        pltpu.VMEM((N_CAND_ROWS, tt), jnp.float32)],
        compiler_params=_params("parallel"), name="peer_prep")(q, keys_hi, keys_lo)


PEER_KEY_GROUP = 2


def _peer_main_kernel(pre_ref, xf_ref, vt_ref, ry_ref, cnt_ref, amp_ref, eb_ref, lg_ref, lb_ref,
                      o_ref, acc_ref, *, rows_per_step):
    e = pl.program_id(1)
    tb = pre_ref.shape[1]
    bf = jnp.bfloat16

    @pl.when(e == 0)
    def _():
        acc_ref[...] = jnp.zeros(acc_ref.shape, jnp.float32)

    def weigh(ii0):
        zero = jnp.zeros((N_KEYS, LANES), bf)
        out = [[] for _ in range(PEER_KEY_GROUP)]
        for tc in range(tb // LANES):
            cols = slice(tc * LANES, (tc + 1) * LANES)
            spread = lambda ref, h, a: jnp.broadcast_to(
                ref[h, tc, pl.ds(e * rows_per_step + ii0 + a, 1), :], (N_KEYS, LANES)).astype(bf)
            gs = [zero] * PEER_KEY_GROUP
            for h in range(PEER_HEADS):
                ry, eb = ry_ref[h, tc], eb_ref[h, tc]
                for a in range(PEER_KEY_GROUP):
                    sel = jnp.maximum(jnp.minimum(spread(cnt_ref, h, a) - ry, eb), zero)
                    gs[a] = gs[a] + spread(amp_ref, h, a) * sel
            for a in range(PEER_KEY_GROUP):
                rows = slice((ii0 + a) * N_KEYS, (ii0 + a + 1) * N_KEYS)
                pre = pre_ref[rows, cols].astype(jnp.float32)
                act = pre * (1.0 + lax.erf(pre * (2.0 ** -0.5)))
                out[a].append(gs[a] * act.astype(bf))
        return jnp.concatenate([jnp.concatenate(tiles, axis=1) for tiles in out], axis=0)

    wgt = jnp.concatenate([weigh(ii0) for ii0 in range(0, rows_per_step, PEER_KEY_GROUP)], axis=0)
    acc_ref[...] += jnp.dot(vt_ref[...], wgt, preferred_element_type=jnp.float32)

    @pl.when(e == pl.num_programs(1) - 1)
    def _():
        z = DN_ALPHA * xf_ref[...] + acc_ref[...]
        mu = jnp.mean(z, axis=0, keepdims=True)
        zc = z - mu
        var = jnp.mean(zc * zc, axis=0, keepdims=True)
        o_ref[...] = (zc * lax.rsqrt(var + LN_EPS)).T * lg_ref[...] + lb_ref[...]


def _peer_main(pre, x1t, vt_bf, ry, cnt, amp, eb, ln_g, ln_b, tb, eb_rows):
    d, t = x1t.shape
    tb = min(tb, t)
    rows_per_step = eb_rows // N_KEYS
    tab = lambda: pl.BlockSpec((PEER_HEADS, tb // LANES, N_KEYS, LANES), lambda i, e: (0, i, 0, 0))
    colv = lambda: pl.BlockSpec((1, d), lambda i, e: (0, 0))
    kern = functools.partial(_peer_main_kernel, rows_per_step=rows_per_step)
    return pl.pallas_call(
        kern, grid=(t // tb, N_EXPERTS // eb_rows),
        in_specs=[pl.BlockSpec((eb_rows, tb), lambda i, e: (e, i)),
                  pl.BlockSpec((d, tb), lambda i, e: (0, i)),
                  pl.BlockSpec((d, eb_rows), lambda i, e: (0, e)),
                  tab(), tab(), tab(), tab(), colv(), colv()],
        out_specs=pl.BlockSpec((tb, d), lambda i, e: (i, 0)),
        out_shape=jax.ShapeDtypeStruct((t, d), jnp.float32),
        scratch_shapes=[pltpu.VMEM((d, tb), jnp.float32)],
        compiler_params=_params("parallel", "arbitrary"),
        name="peer_dense")(pre, x1t, vt_bf, ry, cnt, amp, eb, ln_g.reshape(1, d), ln_b.reshape(1, d))


def _token_mixing(xt, batch, seq, w_in, da_lambda, da_subln_g, ml_conv_w, ml_conv_b, ml_gate_b,
                  ml_norm_g, w_branch_attn, w_branch_mlstm, w_out, ln1_g, ln1_b):
    bf = jnp.bfloat16
    zpad = jnp.zeros((D_MODEL, LANES - N_GATE_COLS), jnp.float32)
    w_cols = jnp.concatenate([w_in[:, IN_MAIN + N_GATE_COLS:], w_in[:, :IN_MAIN],
                              w_in[:, IN_MAIN:IN_MAIN + N_GATE_COLS], zpad], axis=1).astype(bf)
    proj = _matmul(xt, w_cols, *TILE_PROJ, "in_projection")

    qkv = _rope_cast(proj, seq, TILE_ROWS)
    y_attn = _diff_attention(qkv, da_lambda, da_subln_g, batch, seq, *TILE_ATTN)

    qk_conv = _conv_silu(proj, ml_conv_w, ml_conv_b, seq, TILE_ROWS)
    order = jnp.array([0, 2, 1, 3])
    gates_t = proj[:, COL_MLG:COL_MLG + N_GATE_COLS].reshape(batch, seq, 4, ML_HEADS)[:, :, order]
    gates_t = gates_t.reshape(batch, seq, N_GATE_COLS).transpose(0, 2, 1)
    gtab = _gate_scans(gates_t, ml_gate_b[order].reshape(N_GATE_COLS))
    h_fw, h_bw = _mlstm_scan(qk_conv, proj, gtab, batch, seq, min(MLSTM_CHUNKS, seq // ML_CHUNK))

    return _merge(xt, y_attn, h_fw, h_bw, proj, ml_norm_g,
                  w_branch_attn.astype(bf), w_branch_mlstm.astype(bf), w_out.astype(bf),
                  ln1_g, ln1_b, TILE_MERGE)


def _peer_layer(x1, x1t, x1t_bf, peer_w_q, peer_sub_keys, peer_u, peer_v, ln2_g, ln2_b):
    bf = jnp.bfloat16
    wq_hi, wq_lo = _split_bf16(peer_w_q)
    q = _matmul3(x1, wq_hi, wq_lo, *TILE_QPROJ)
    k_hi, k_lo = _split_bf16(peer_sub_keys)
    ry, cnt, amp, eb = _peer_prep(q, k_hi, k_lo, TILE_PREP)
    pre = _matmul(peer_u.astype(bf), x1t_bf, *TILE_PREACT, "peer_preact", bf)
    return _peer_main(pre, x1t, peer_v.T.astype(bf), ry, cnt, amp, eb, ln2_g, ln2_b, *TILE_PEER)


def kernel(x, w_in, da_lambda, da_subln_g, ml_conv_w, ml_conv_b, ml_gate_b, ml_norm_g,
           w_branch_attn, w_branch_mlstm, w_out, ln1_g, ln1_b,
           peer_w_q, peer_sub_keys, peer_u, peer_v, ln2_g, ln2_b):
    batch, seq, d = x.shape
    xt = x.reshape(batch * seq, d)
    x1, x1t, x1t_bf = _token_mixing(xt, batch, seq, w_in[0], da_lambda[0], da_subln_g[0], ml_conv_w[0],
                                    ml_conv_b[0], ml_gate_b[0], ml_norm_g[0], w_branch_attn[0],
                                    w_branch_mlstm[0], w_out[0], ln1_g[0], ln1_b[0])
    out = _peer_layer(x1, x1t, x1t_bf, peer_w_q[0], peer_sub_keys[0], peer_u[0], peer_v[0],
                      ln2_g[0], ln2_b[0])
    return out.reshape(batch, seq, d)
```

```python
import functools
import math

import jax
import jax.numpy as jnp
from jax import lax
from jax.experimental import pallas as pl
from jax.experimental.pallas import tpu as pltpu

D_MODEL = 1024
DA_QK_DIM = 64
DA_V_DIM = 128
DA_HEADS = 4
DA_WIDTH = 512
ROPE_DIM = 16
ROPE_THETA = 500000.0
ML_WIDTH = 512
ML_HEADS = 4
ML_HEAD_DIM = 128
ML_CONV = 5
ML_CHUNK = 128
STAB_INIT = -1e30
N_KEYS = 128
N_EXPERTS = N_KEYS * N_KEYS
PEER_HEADS = 8
PEER_TOPK = 16
DEPTH = 1
DN_ALPHA = (2 * DEPTH) ** 0.25
LN_EPS = 1e-5
LAMBDA_INIT = 0.8 - 0.6 * math.exp(-0.3 * 0)

LANES = 128
NEG_INF = float("-inf")
VMEM_LIMIT = 48 * 1024 * 1024

TILE_PROJ = (1024, 1920)
TILE_ROWS = 2048
TILE_ATTN = (256, 4096)
MLSTM_CHUNKS = 4
TILE_MERGE = 512
TILE_QPROJ = (1024, 1024)
TILE_PREP = 256
TILE_PREACT = (1024, 1024)
TILE_PEER = (512, 2048)

COL_BR = 0
COL_Q, COL_K, COL_V = 2048, 2560, 3072
COL_MLQ, COL_MLK, COL_MLV, COL_MLO = 3584, 4096, 4608, 5120
COL_MLG = 5632
N_GATE_COLS = 4 * ML_HEADS
IN_MAIN = 3584


def _params(*sem):
    return pltpu.CompilerParams(dimension_semantics=sem, vmem_limit_bytes=VMEM_LIMIT)


def _mm_kernel(x_ref, w_ref, o_ref):
    o_ref[...] = jnp.dot(x_ref[...].astype(jnp.bfloat16), w_ref[...],
                         preferred_element_type=jnp.float32).astype(o_ref.dtype)


def _matmul(x, w, tm, tn, name, out_dtype=jnp.float32):
    m, k = x.shape
    n = w.shape[1]
    tm, tn = min(tm, m), min(tn, n)
    return pl.pallas_call(
        _mm_kernel, grid=(m // tm, n // tn),
        in_specs=[pl.BlockSpec((tm, k), lambda i, j: (i, 0)),
                  pl.BlockSpec((k, tn), lambda i, j: (0, j))],
        out_specs=pl.BlockSpec((tm, tn), lambda i, j: (i, j)),
        out_shape=jax.ShapeDtypeStruct((m, n), out_dtype),
        compiler_params=_params("parallel", "parallel"), name=name)(x, w)


def _split_bf16(a):
    hi = a.astype(jnp.bfloat16)
    lo = (a - hi.astype(jnp.float32)).astype(jnp.bfloat16)
    return hi, lo


def _dot3(a_hi, a_lo, b_hi, b_lo, dims):
    f = lambda a, b: lax.dot_general(a, b, (dims, ((), ())), preferred_element_type=jnp.float32)
    return f(a_hi, b_hi) + f(a_hi, b_lo) + f(a_lo, b_hi)


def _mm3_kernel(x_ref, wh_ref, wl_ref, o_ref):
    xh, xl = _split_bf16(x_ref[...])
    o_ref[...] = _dot3(xh, xl, wh_ref[...], wl_ref[...], ((1,), (0,)))


def _matmul3(x, w_hi, w_lo, tm, tn):
    m, k = x.shape
    n = w_hi.shape[1]
    tm, tn = min(tm, m), min(tn, n)
    return pl.pallas_call(
        _mm3_kernel, grid=(m // tm, n // tn),
        in_specs=[pl.BlockSpec((tm, k), lambda i, j: (i, 0)),
                  pl.BlockSpec((k, tn), lambda i, j: (0, j)),
                  pl.BlockSpec((k, tn), lambda i, j: (0, j))],
        out_specs=pl.BlockSpec((tm, tn), lambda i, j: (i, j)),
        out_shape=jax.ShapeDtypeStruct((m, n), jnp.float32),
        compiler_params=_params("parallel", "parallel"), name="dense_matmul3")(x, w_hi, w_lo)


def _rope_kernel(q_ref, k_ref, v_ref, c_ref, s1_ref, s2_ref, o_ref):
    c, s1, s2 = c_ref[...], s1_ref[...], s2_ref[...]
    for which, src in enumerate((q_ref, k_ref)):
        for hd in range(DA_HEADS):
            x = src[:, hd * LANES:(hd + 1) * LANES]
            r = x * c + pltpu.roll(x, LANES - 8, 1) * s1 + pltpu.roll(x, 8, 1) * s2
            if which == 0:
                r = r * (DA_QK_DIM ** -0.5)
            col = which * DA_WIDTH + hd * LANES
            o_ref[:, col:col + LANES] = r.astype(jnp.bfloat16)
    o_ref[:, 2 * DA_WIDTH:3 * DA_WIDTH] = v_ref[...].astype(jnp.bfloat16)


def _rope_tables(seq):
    pos = jnp.arange(seq, dtype=jnp.float32)
    inv_freq = ROPE_THETA ** (-jnp.arange(0, ROPE_DIM, 2, dtype=jnp.float32) / ROPE_DIM)
    ang = pos[:, None] * inv_freq[None, :]
    cos, sin = jnp.cos(ang), jnp.sin(ang)
    half = ROPE_DIM // 2
    zeros = jnp.zeros((seq, DA_QK_DIM - ROPE_DIM), jnp.float32)
    z8 = jnp.zeros((seq, half), jnp.float32)
    c64 = jnp.concatenate([cos, cos, zeros + 1.0], -1)
    s1_64 = jnp.concatenate([-sin, z8, zeros], -1)
    s2_64 = jnp.concatenate([z8, sin, zeros], -1)
    tile2 = lambda t: jnp.concatenate([t, t], -1)
    return tile2(c64), tile2(s1_64), tile2(s2_64)


def _rope_cast(proj, seq, tm):
    t = proj.shape[0]
    tm = min(tm, seq)
    nsb = seq // tm
    c, s1, s2 = _rope_tables(seq)
    tab = pl.BlockSpec((tm, LANES), lambda i: (i % nsb, 0))
    src = lambda col: pl.BlockSpec((tm, DA_WIDTH), lambda i: (i, col // DA_WIDTH))
    return pl.pallas_call(
        _rope_kernel, grid=(t // tm,),
        in_specs=[src(COL_Q), src(COL_K), src(COL_V), tab, tab, tab],
        out_specs=pl.BlockSpec((tm, 3 * DA_WIDTH), lambda i: (i, 0)),
        out_shape=jax.ShapeDtypeStruct((t, 3 * DA_WIDTH), jnp.bfloat16),
        compiler_params=_params("parallel"), name="rope_cast")(proj, proj, proj, c, s1, s2)


def _attn_kernel(lam_ref, g_ref, q_ref, k_ref, v_ref, o_ref, m_ref, acc_ref, sa_ref, sb_ref, *, tk):
    nk = k_ref.shape[0] // tk
    m_ref[...] = jnp.full(m_ref.shape, NEG_INF, jnp.float32)
    acc_ref[...] = jnp.zeros(acc_ref.shape, jnp.float32)
    lane = lax.broadcasted_iota(jnp.int32, (tk, LANES), 1)
    ones_col = jnp.where(lane == 0, 1.0, 0.0).astype(jnp.bfloat16)

    def scores(j, s_ref):
        k = k_ref[pl.ds(pl.multiple_of(j * tk, tk), tk), :]
        for mp in range(2):
            q = q_ref[:, mp * DA_QK_DIM:(mp + 1) * DA_QK_DIM]
            s_ref[mp] = lax.dot_general(q, k[:, mp * DA_QK_DIM:(mp + 1) * DA_QK_DIM],
                                        (((1,), (1,)), ((), ())), preferred_element_type=jnp.float32)

    def accumulate(j, s_ref):
        vext = jnp.concatenate([v_ref[pl.ds(pl.multiple_of(j * tk, tk), tk), :], ones_col], axis=1)
        for mp in range(2):
            s = s_ref[mp]
            m_old = m_ref[mp]
            m_new = jnp.maximum(m_old, jnp.max(s, axis=-1, keepdims=True))
            p = jnp.exp(s - m_new).astype(jnp.bfloat16)
            pv = jnp.dot(p, vext, preferred_element_type=jnp.float32)
            acc_ref[mp] = jnp.exp(m_old - m_new) * acc_ref[mp] + pv
            m_ref[mp] = m_new

    scores(0, sa_ref)

    def body(jj, carry):
        j = 2 * jj
        scores(j + 1, sb_ref)
        accumulate(j, sa_ref)
        scores(j + 2, sa_ref)
        accumulate(j + 1, sb_ref)
        return carry

    lax.fori_loop(0, nk // 2 - 1, body, 0)
    scores(nk - 1, sb_ref)
    accumulate(nk - 2, sa_ref)
    accumulate(nk - 1, sb_ref)

    lp = lam_ref[...]
    lam = (jnp.exp(jnp.sum(lp[0:1] * lp[1:2], axis=-1, keepdims=True))
           - jnp.exp(jnp.sum(lp[2:3] * lp[3:4], axis=-1, keepdims=True)) + LAMBDA_INIT)
    a0, a1 = acc_ref[0], acc_ref[1]
    o = (a0[:, :DA_V_DIM] / a0[:, DA_V_DIM:DA_V_DIM + 1]
         - lam * (a1[:, :DA_V_DIM] / a1[:, DA_V_DIM:DA_V_DIM + 1]))
    y = o * lax.rsqrt(jnp.mean(o * o, axis=-1, keepdims=True) + LN_EPS)
    o_ref[...] = y * g_ref[...] * (1.0 - LAMBDA_INIT)


def _diff_attention(qkv, da_lambda, subln_g, batch, seq, tq, tk):
    t = qkv.shape[0]
    tq, tk = min(tq, seq), min(tk, seq // 2)
    assert (seq // tk) % 2 == 0, "the key loop is software-pipelined over chunk pairs"
    nq = seq // tq
    kern = functools.partial(_attn_kernel, tk=tk)
    return pl.pallas_call(
        kern, grid=(batch, DA_HEADS, nq),
        in_specs=[pl.BlockSpec((4, DA_QK_DIM), lambda b, h, i: (0, 0)),
                  pl.BlockSpec((1, DA_V_DIM), lambda b, h, i: (0, 0)),
                  pl.BlockSpec((tq, LANES), lambda b, h, i: (b * nq + i, h)),
                  pl.BlockSpec((seq, LANES), lambda b, h, i: (b, DA_HEADS + h)),
                  pl.BlockSpec((seq, LANES), lambda b, h, i: (b, 2 * DA_HEADS + h))],
        out_specs=pl.BlockSpec((tq, LANES), lambda b, h, i: (b * nq + i, h)),
        out_shape=jax.ShapeDtypeStruct((t, DA_WIDTH), jnp.float32),
        scratch_shapes=[pltpu.VMEM((2, tq, 1), jnp.float32),
                        pltpu.VMEM((2, tq, 2 * LANES), jnp.float32),
                        pltpu.VMEM((2, tq, tk), jnp.float32),
                        pltpu.VMEM((2, tq, tk), jnp.float32)],
        compiler_params=_params("parallel", "parallel", "parallel"),
        name="diff_attention")(da_lambda, subln_g.reshape(1, DA_V_DIM), qkv, qkv, qkv)


def _conv_kernel(prev_ref, cur_ref, next_ref, w_ref, b_ref, o_ref, *, nsb):
    i, c = pl.program_id(0), pl.program_id(1)
    tm = cur_ref.shape[0]
    sb = i % nsb
    prev = jnp.where(sb == 0, 0.0, prev_ref[...])
    nxt = jnp.where(sb == nsb - 1, 0.0, next_ref[...])
    ext = jnp.concatenate([prev, cur_ref[...], nxt], axis=0)
    n = tm + 16
    pad = (ML_CONV - 1) // 2
    acc = jnp.zeros((tm, cur_ref.shape[1]), jnp.float32) + b_ref[...]
    for w in range(ML_CONV):
        shift = (pad - w) % n
        z = ext if shift == 0 else pltpu.roll(ext, shift, 0)
        acc = acc + z[8:8 + tm] * w_ref[w:w + 1, :]
    y = acc * jax.nn.sigmoid(acc)
    y = y * jnp.where(c == 1, ML_HEAD_DIM ** -0.5, 1.0)
    o_ref[...] = y.astype(jnp.bfloat16)


def _conv_silu(proj, conv_w, conv_b, seq, tm):
    t = proj.shape[0]
    tm = min(tm, seq)
    nsb = seq // tm
    r8 = tm // 8
    nrow8 = t // 8
    cb0 = COL_MLQ // ML_WIDTH
    kern = functools.partial(_conv_kernel, nsb=nsb)
    return pl.pallas_call(
        kern, grid=(t // tm, 2),
        in_specs=[pl.BlockSpec((8, ML_WIDTH), lambda i, c: (jnp.maximum(i * r8 - 1, 0), cb0 + c)),
                  pl.BlockSpec((tm, ML_WIDTH), lambda i, c: (i, cb0 + c)),
                  pl.BlockSpec((8, ML_WIDTH), lambda i, c: (jnp.minimum((i + 1) * r8, nrow8 - 1), cb0 + c)),
                  pl.BlockSpec((ML_CONV, ML_WIDTH), lambda i, c: (0, c)),
                  pl.BlockSpec((1, ML_WIDTH), lambda i, c: (0, c))],
        out_specs=pl.BlockSpec((tm, ML_WIDTH), lambda i, c: (i, c)),
        out_shape=jax.ShapeDtypeStruct((t, 2 * ML_WIDTH), jnp.bfloat16),
        compiler_params=_params("parallel", "parallel"),
        name="conv_silu")(proj, proj, proj, conv_w, conv_b.reshape(1, 2 * ML_WIDTH))


def _seg_scan(x, pos, seq, op, fill, reverse):
    y = x
    d = 1
    while d < ML_CHUNK:
        if reverse:
            y = op(y, jnp.where(pos < ML_CHUNK - d, pltpu.roll(y, seq - d, 1), fill))
        else:
            y = op(y, jnp.where(pos >= d, pltpu.roll(y, d, 1), fill))
        d *= 2
    return y


def _gate_dir(i_pre, logf, pos, lane, seq, reverse):
    bj = _seg_scan(logf, pos, seq, jnp.add, 0.0, reverse)
    last = (pos == 0) if reverse else (pos == ML_CHUNK - 1)
    b_tot = _seg_scan(jnp.where(last, bj, NEG_INF), pos, seq, jnp.maximum, NEG_INF, not reverse)
    w = i_pre - bj
    cmax_w = _seg_scan(w, pos, seq, jnp.maximum, NEG_INF, reverse)
    g = b_tot - bj + i_pre
    g_max = jnp.maximum(_seg_scan(g, pos, seq, jnp.maximum, NEG_INF, False),
                        _seg_scan(g, pos, seq, jnp.maximum, NEG_INF, True))
    acc_a, acc_g = b_tot, g_max
    d = ML_CHUNK
    while d < seq:
        if reverse:
            ok = lane < seq - d
            pa = jnp.where(ok, pltpu.roll(acc_a, seq - d, 1), 0.0)
            pg = jnp.where(ok, pltpu.roll(acc_g, seq - d, 1), NEG_INF)
        else:
            ok = lane >= d
            pa = jnp.where(ok, pltpu.roll(acc_a, d, 1), 0.0)
            pg = jnp.where(ok, pltpu.roll(acc_g, d, 1), NEG_INF)
        acc_g = jnp.maximum(pg + acc_a, acc_g)
        acc_a = pa + acc_a
        d *= 2
    m_after = jnp.maximum(acc_a + STAB_INIT, acc_g)
    if reverse:
        m_before = jnp.where(lane < seq - ML_CHUNK, pltpu.roll(m_after, seq - ML_CHUNK, 1), STAB_INIT)
    else:
        m_before = jnp.where(lane >= ML_CHUNK, pltpu.roll(m_after, ML_CHUNK, 1), STAB_INIT)
    mr_rel = jnp.maximum(m_before, cmax_w)
    m_row = bj + mr_rel
    u = -mr_rel
    w_inter = jnp.exp(m_before - mr_rel)
    e_negm = jnp.exp(-m_row)
    wk = jnp.exp(g - m_after)
    a = jnp.exp(b_tot + m_before - m_after)
    return u, w, w_inter, e_negm, wk, a


def _gate_kernel(g_ref, b_ref, o_ref):
    seq = g_ref.shape[2]
    g = g_ref[0] + b_ref[...]
    i_pre, f_pre = g[0:8], g[8:16]
    logf = jnp.minimum(f_pre, 0.0) - jnp.log1p(jnp.exp(-jnp.abs(f_pre)))
    lane = lax.broadcasted_iota(jnp.int32, (2 * ML_HEADS, seq), 1)
    row = lax.broadcasted_iota(jnp.int32, (2 * ML_HEADS, seq), 0)
    pos = lane % ML_CHUNK
    fwd = _gate_dir(i_pre, logf, pos, lane, seq, False)
    bwd = _gate_dir(i_pre, logf, pos, lane, seq, True)
    for q, (af, ab) in enumerate(zip(fwd, bwd)):
        o_ref[0, q] = jnp.where(row < ML_HEADS, af, ab)


def _gate_scans(gates_t, gate_b):
    b, _, seq = gates_t.shape
    return pl.pallas_call(
        _gate_kernel, grid=(b,),
        in_specs=[pl.BlockSpec((1, 16, seq), lambda i: (i, 0, 0)),
                  pl.BlockSpec((16, 1), lambda i: (0, 0))],
        out_specs=pl.BlockSpec((1, 6, 8, seq), lambda i: (i, 0, 0, 0)),
        out_shape=jax.ShapeDtypeStruct((b, 6, 8, seq), jnp.float32),
        compiler_params=_params("parallel"), name="mlstm_gate_scans")(gates_t, gate_b.reshape(16, 1))


Q_U, Q_W, Q_WINTER, Q_ENEGM, Q_WK, Q_A = range(6)


def _mlstm_kernel(qf_ref, kf_ref, vf_ref, gf_ref, qb_ref, kb_ref, vb_ref, gb_ref,
                  of_ref, ob_ref, ce_ref, *, chunks):
    @pl.when(pl.program_id(1) == 0)
    def _():
        ce_ref[...] = jnp.zeros(ce_ref.shape, jnp.float32)

    L = ML_CHUNK
    row = lax.broadcasted_iota(jnp.int32, (L, L), 0)
    col = lax.broadcasted_iota(jnp.int32, (L, L), 1)
    lane = lax.broadcasted_iota(jnp.int32, (L, LANES), 1)
    ones_col = jnp.where(lane == 0, 1.0, 0.0).astype(jnp.bfloat16)
    dirs = ((0, qf_ref, kf_ref, vf_ref, gf_ref, of_ref, col <= row, range(chunks)),
            (1, qb_ref, kb_ref, vb_ref, gb_ref, ob_ref, col >= row, range(chunks - 1, -1, -1)))
    for d, q_ref, k_ref, v_ref, g_ref, o_ref, mask, order in dirs:
        n_q, n_r, rows = g_ref.shape[1:]
        cols = g_ref[0].reshape(n_q * n_r, rows).T
        for c in order:
            r0 = c * L
            for h in range(ML_HEADS):
                hs = slice(h * ML_HEAD_DIM, (h + 1) * ML_HEAD_DIM)
                cidx = lambda qn: cols[r0:r0 + L, qn * 8 + d * 4 + h:qn * 8 + d * 4 + h + 1]
                q = q_ref[r0:r0 + L, hs]
                k = k_ref[r0:r0 + L, hs]
                vext = jnp.concatenate([v_ref[r0:r0 + L, hs].astype(jnp.bfloat16), ones_col], axis=1)
                s = lax.dot_general(q, k, (((1,), (1,)), ((), ())), preferred_element_type=jnp.float32)
                w_row = g_ref[0, Q_W, d * 4 + h:d * 4 + h + 1, r0:r0 + L]
                dec = jnp.where(mask, jnp.exp(cidx(Q_U) + w_row), 0.0)
                sd = (s * dec).astype(jnp.bfloat16)
                ce = ce_ref[d * 4 + h]
                inter = jnp.dot(q, ce.astype(jnp.bfloat16), preferred_element_type=jnp.float32)
                intra = jnp.dot(sd, vext, preferred_element_type=jnp.float32)
                tot = cidx(Q_WINTER) * inter + intra
                den = jnp.maximum(jnp.abs(tot[:, ML_HEAD_DIM:ML_HEAD_DIM + 1]), cidx(Q_ENEGM))
                o_ref[r0:r0 + L, hs] = tot[:, :ML_HEAD_DIM] / den
                kw_t = (k.astype(jnp.float32) * cidx(Q_WK)).T.astype(jnp.bfloat16)
                upd = jnp.dot(kw_t, vext, preferred_element_type=jnp.float32)
                a = cols[r0:r0 + 1, Q_A * 8 + d * 4 + h:Q_A * 8 + d * 4 + h + 1]
                ce_ref[d * 4 + h] = a * ce + upd


def _mlstm_scan(qk_conv, proj, gtab, batch, seq, chunks):
    t = qk_conv.shape[0]
    rows = chunks * ML_CHUNK
    ns = seq // rows
    vcb = COL_MLV // ML_WIDTH
    fw = lambda b, j: b * ns + j
    bw = lambda b, j: b * ns + ns - 1 - j
    def specs(rb, lb):
        return [pl.BlockSpec((rows, ML_WIDTH), lambda b, j: (rb(b, j), 0)),
                pl.BlockSpec((rows, ML_WIDTH), lambda b, j: (rb(b, j), 1)),
                pl.BlockSpec((rows, ML_WIDTH), lambda b, j: (rb(b, j), vcb)),
                pl.BlockSpec((1,) + gtab.shape[1:3] + (rows,), lambda b, j: (b, 0, 0, lb(j)))]
    out_f = pl.BlockSpec((rows, ML_WIDTH), lambda b, j: (fw(b, j), 0))
    out_b = pl.BlockSpec((rows, ML_WIDTH), lambda b, j: (bw(b, j), 0))
    kern = functools.partial(_mlstm_kernel, chunks=chunks)
    return pl.pallas_call(
        kern, grid=(batch, ns),
        in_specs=specs(fw, lambda j: j) + specs(bw, lambda j: ns - 1 - j),
        out_specs=[out_f, out_b],
        out_shape=[jax.ShapeDtypeStruct((t, ML_WIDTH), jnp.float32)] * 2,
        scratch_shapes=[pltpu.VMEM((2 * ML_HEADS, ML_HEAD_DIM, 2 * LANES), jnp.float32)],
        compiler_params=_params("parallel", "arbitrary"),
        name="mlstm_scan")(qk_conv, qk_conv, proj, gtab, qk_conv, qk_conv, proj, gtab)


def _layer_norm_rows(z, g, b):
    mu = jnp.mean(z, axis=-1, keepdims=True)
    zc = z - mu
    var = jnp.mean(zc * zc, axis=-1, keepdims=True)
    return zc * lax.rsqrt(var + LN_EPS) * g + b


def _merge_kernel(x_ref, ya_ref, hf_ref, hb_ref, mo_ref, g0_ref, g1_ref, ng_ref,
                  wa_ref, wm_ref, wo_ref, lg_ref, lb_ref, o_ref, ot_ref, otb_ref):
    h = hf_ref[...] + hb_ref[...]
    parts = []
    for hd in range(ML_HEADS):
        hh = h[:, hd * ML_HEAD_DIM:(hd + 1) * ML_HEAD_DIM]
        mu = jnp.mean(hh, axis=-1, keepdims=True)
        hc = hh - mu
        var = jnp.mean(hc * hc, axis=-1, keepdims=True)
        parts.append(hc * lax.rsqrt(var + LN_EPS))
    ym = jnp.concatenate(parts, axis=1) * ng_ref[...] * jax.nn.sigmoid(mo_ref[...])
    pa = jnp.dot(ya_ref[...].astype(jnp.bfloat16), wa_ref[...], preferred_element_type=jnp.float32)
    pm = jnp.dot(ym.astype(jnp.bfloat16), wm_ref[...], preferred_element_type=jnp.float32)
    merged = jax.nn.sigmoid(g0_ref[...]) * pa + jax.nn.sigmoid(g1_ref[...]) * pm
    mix = jnp.dot(merged.astype(jnp.bfloat16), wo_ref[...], preferred_element_type=jnp.float32)
    x1 = _layer_norm_rows(DN_ALPHA * x_ref[...] + mix, lg_ref[...], lb_ref[...])
    o_ref[...] = x1
    x1t = x1.T
    ot_ref[...] = x1t
    otb_ref[...] = x1t.astype(jnp.bfloat16)


def _merge(xt, y_attn, h_fw, h_bw, proj, norm_g, wa, wm, wo, ln_g, ln_b, tm):
    t = xt.shape[0]
    tm = min(tm, t)
    row = lambda w, cb: pl.BlockSpec((tm, w), lambda i: (i, cb))
    full = lambda a: pl.BlockSpec(a.shape, lambda i: (0, 0))
    vec = lambda a: a.reshape(1, -1)
    args = (xt, y_attn, h_fw, h_bw, proj, proj, proj, vec(norm_g), wa, wm, wo, vec(ln_g), vec(ln_b))
    in_specs = [row(D_MODEL, 0), row(DA_WIDTH, 0), row(ML_WIDTH, 0), row(ML_WIDTH, 0),
                row(ML_WIDTH, COL_MLO // ML_WIDTH),
                row(D_MODEL, COL_BR // D_MODEL), row(D_MODEL, COL_BR // D_MODEL + 1)]
    in_specs += [full(a) for a in args[7:]]
    return pl.pallas_call(
        _merge_kernel, grid=(t // tm,), in_specs=in_specs,
        out_specs=[pl.BlockSpec((tm, D_MODEL), lambda i: (i, 0)),
                   pl.BlockSpec((D_MODEL, tm), lambda i: (0, i)),
                   pl.BlockSpec((D_MODEL, tm), lambda i: (0, i))],
        out_shape=[jax.ShapeDtypeStruct((t, D_MODEL), jnp.float32),
                   jax.ShapeDtypeStruct((D_MODEL, t), jnp.float32),
                   jax.ShapeDtypeStruct((D_MODEL, t), jnp.bfloat16)],
        compiler_params=_params("parallel"), name="merge_ln")(*args)


NOT_TOP = float(PEER_TOPK)
N_CAND_ROWS = 80


def _topk_rows(s, dst_ref, want_rank):
    rank = jnp.full(s.shape, NOT_TOP, jnp.float32) if want_rank else None
    for x in range(PEER_TOPK):
        m = jnp.max(s, axis=0, keepdims=True)
        dst_ref[x:x + 1, :] = m
        hit = s == m
        if want_rank:
            rank = jnp.where(hit, float(x), rank)
        s = jnp.where(hit, NEG_INF, s)
    return rank


def _peer_prep_kernel(q_ref, kh_ref, kl_ref, ry_ref, cnt_ref, amp_ref, eb_ref, a_ref, b_ref, sum_ref):
    for head in range(PEER_HEADS):
        _peer_prep_head(head, q_ref, kh_ref, kl_ref, ry_ref, cnt_ref, amp_ref, eb_ref, a_ref, b_ref, sum_ref)


def _peer_prep_head(head, q_ref, kh_ref, kl_ref, ry_ref, cnt_ref, amp_ref, eb_ref, a_ref, b_ref, sum_ref):
    half = N_KEYS
    sc = []
    for p in range(2):
        col = (2 * head + p) * half
        qh, ql = _split_bf16(q_ref[:, col:col + half])
        sc.append(_dot3(kh_ref[p], kl_ref[p], qh, ql, ((1,), (1,))))
    s1, s2 = sc
    _topk_rows(s1, a_ref, False)
    rank2 = _topk_rows(s2, b_ref, True)
    a, b = a_ref[...], b_ref[...]
    h8 = PEER_TOPK // 2
    sum_ref[0:PEER_TOPK, :] = a + b[0:1, :]
    for y in range(1, h8):
        sum_ref[PEER_TOPK + (y - 1) * h8:PEER_TOPK + y * h8, :] = a[0:h8, :] + b[y:y + 1, :]
    sum_ref[N_CAND_ROWS - h8:N_CAND_ROWS, :] = b[h8:PEER_TOPK, :] + a[0:1, :]
    sums = sum_ref[...]
    rest = sums
    for _ in range(PEER_TOPK):
        tau = jnp.max(rest, axis=0, keepdims=True)
        rest = jnp.where(rest == tau, NEG_INF, rest)
    top = a[0:1, :] + b[0:1, :]
    z = jnp.sum(jnp.where(sums >= tau, jnp.exp(sums - top), 0.0), axis=0, keepdims=True)
    cntx = jnp.zeros(a.shape, jnp.float32)
    for y in range(PEER_TOPK):
        cntx = cntx + jnp.where(a + b[y:y + 1, :] >= tau, 1.0, 0.0)
    cnt = jnp.zeros(s1.shape, jnp.float32)
    for x in range(PEER_TOPK):
        cnt = jnp.where(s1 == a[x:x + 1, :], cntx[x:x + 1, :], cnt)
    results = ((ry_ref, rank2.astype(jnp.bfloat16)),
               (cnt_ref, cnt),
               (amp_ref, 0.5 * jnp.exp(s1 - a[0:1, :])),
               (eb_ref, (jnp.exp(s2 - b[0:1, :]) / z).astype(jnp.bfloat16)))
    for ref, val in results:
        for tc in range(ref.shape[1]):
            ref[head, tc] = val[:, tc * LANES:(tc + 1) * LANES]


def _peer_prep(q, keys_hi, keys_lo, tt):
    t = q.shape[0]
    tt = min(tt, t)
    qdim = PEER_HEADS * 2 * N_KEYS
    shape = (PEER_HEADS, t // LANES, N_KEYS, LANES)
    outs = [jax.ShapeDtypeStruct(shape, dt) for dt in (jnp.bfloat16, jnp.float32, jnp.float32, jnp.bfloat16)]
    ospec = pl.BlockSpec((PEER_HEADS, tt // LANES, N_KEYS, LANES), lambda i: (0, i, 0, 0))
    kspec = pl.BlockSpec((2, N_KEYS, N_KEYS), lambda i: (0, 0, 0))
    return pl.pallas_call(
        _peer_prep_kernel, grid=(t // tt,),
        in_specs=[pl.BlockSpec((tt, qdim), lambda i: (i, 0)), kspec, kspec],
        out_specs=[ospec] * 4, out_shape=outs,
        scratch_shapes=[pltpu.VMEM((PEER_TOPK, tt), jnp.float32),
                        pltpu.VMEM((PEER_TOPK, tt), jnp.float32),
                        pltpu.VMEM((N_CAND_ROWS, tt), jnp.float32)],
        compiler_params=_params("parallel"), name="peer_prep")(q, keys_hi, keys_lo)


PEER_KEY_GROUP = 2


def _peer_main_kernel(pre_ref, xf_ref, vt_ref, ry_ref, cnt_ref, amp_ref, eb_ref, lg_ref, lb_ref,
                      o_ref, acc_ref, *, rows_per_step):
    e = pl.program_id(1)
    tb = pre_ref.shape[1]
    bf = jnp.bfloat16

    @pl.when(e == 0)
    def _():
        acc_ref[...] = jnp.zeros(acc_ref.shape, jnp.float32)

    def weigh(ii0):
        zero = jnp.zeros((N_KEYS, LANES), bf)
        out = [[] for _ in range(PEER_KEY_GROUP)]
        for tc in range(tb // LANES):
            cols = slice(tc * LANES, (tc + 1) * LANES)
            spread = lambda ref, h, a: jnp.broadcast_to(
                ref[h, tc, pl.ds(e * rows_per_step + ii0 + a, 1), :], (N_KEYS, LANES)).astype(bf)
            gs = [zero] * PEER_KEY_GROUP
            for h in range(PEER_HEADS):
                ry, eb = ry_ref[h, tc], eb_ref[h, tc]
                for a in range(PEER_KEY_GROUP):
                    sel = jnp.maximum(jnp.minimum(spread(cnt_ref, h, a) - ry, eb), zero)
                    gs[a] = gs[a] + spread(amp_ref, h, a) * sel
            for a in range(PEER_KEY_GROUP):
                rows = slice((ii0 + a) * N_KEYS, (ii0 + a + 1) * N_KEYS)
                pre = pre_ref[rows, cols].astype(jnp.float32)
                act = pre * (1.0 + lax.erf(pre * (2.0 ** -0.5)))
                out[a].append(gs[a] * act.astype(bf))
        return jnp.concatenate([jnp.concatenate(tiles, axis=1) for tiles in out], axis=0)

    wgt = jnp.concatenate([weigh(ii0) for ii0 in range(0, rows_per_step, PEER_KEY_GROUP)], axis=0)
    acc_ref[...] += jnp.dot(vt_ref[...], wgt, preferred_element_type=jnp.float32)

    @pl.when(e == pl.num_programs(1) - 1)
    def _():
        z = DN_ALPHA * xf_ref[...] + acc_ref[...]
        mu = jnp.mean(z, axis=0, keepdims=True)
        zc = z - mu
        var = jnp.mean(zc * zc, axis=0, keepdims=True)
        o_ref[...] = (zc * lax.rsqrt(var + LN_EPS)).T * lg_ref[...] + lb_ref[...]


def _peer_main(pre, x1t, vt_bf, ry, cnt, amp, eb, ln_g, ln_b, tb, eb_rows):
    d, t = x1t.shape
    tb = min(tb, t)
    rows_per_step = eb_rows // N_KEYS
    tab = lambda: pl.BlockSpec((PEER_HEADS, tb // LANES, N_KEYS, LANES), lambda i, e: (0, i, 0, 0))
    colv = lambda: pl.BlockSpec((1, d), lambda i, e: (0, 0))
    kern = functools.partial(_peer_main_kernel, rows_per_step=rows_per_step)
    return pl.pallas_call(
        kern, grid=(t // tb, N_EXPERTS // eb_rows),
        in_specs=[pl.BlockSpec((eb_rows, tb), lambda i, e: (e, i)),
                  pl.BlockSpec((d, tb), lambda i, e: (0, i)),
                  pl.BlockSpec((d, eb_rows), lambda i, e: (0, e)),
                  tab(), tab(), tab(), tab(), colv(), colv()],
        out_specs=pl.BlockSpec((tb, d), lambda i, e: (i, 0)),
        out_shape=jax.ShapeDtypeStruct((t, d), jnp.float32),
        scratch_shapes=[pltpu.VMEM((d, tb), jnp.float32)],
        compiler_params=_params("parallel", "arbitrary"),
        name="peer_dense")(pre, x1t, vt_bf, ry, cnt, amp, eb, ln_g.reshape(1, d), ln_b.reshape(1, d))


def _token_mixing(xt, batch, seq, w_in, da_lambda, da_subln_g, ml_conv_w, ml_conv_b, ml_gate_b,
                  ml_norm_g, w_branch_attn, w_branch_mlstm, w_out, ln1_g, ln1_b):
    bf = jnp.bfloat16
    zpad = jnp.zeros((D_MODEL, LANES - N_GATE_COLS), jnp.float32)
    w_cols = jnp.concatenate([w_in[:, IN_MAIN + N_GATE_COLS:], w_in[:, :IN_MAIN],
                              w_in[:, IN_MAIN:IN_MAIN + N_GATE_COLS], zpad], axis=1).astype(bf)
    proj = _matmul(xt, w_cols, *TILE_PROJ, "in_projection")

    qkv = _rope_cast(proj, seq, TILE_ROWS)
    y_attn = _diff_attention(qkv, da_lambda, da_subln_g, batch, seq, *TILE_ATTN)

    qk_conv = _conv_silu(proj, ml_conv_w, ml_conv_b, seq, TILE_ROWS)
    order = jnp.array([0, 2, 1, 3])
    gates_t = proj[:, COL_MLG:COL_MLG + N_GATE_COLS].reshape(batch, seq, 4, ML_HEADS)[:, :, order]
    gates_t = gates_t.reshape(batch, seq, N_GATE_COLS).transpose(0, 2, 1)
    gtab = _gate_scans(gates_t, ml_gate_b[order].reshape(N_GATE_COLS))
    h_fw, h_bw = _mlstm_scan(qk_conv, proj, gtab, batch, seq, min(MLSTM_CHUNKS, seq // ML_CHUNK))

    return _merge(xt, y_attn, h_fw, h_bw, proj, ml_norm_g,
                  w_branch_attn.astype(bf), w_branch_mlstm.astype(bf), w_out.astype(bf),
                  ln1_g, ln1_b, TILE_MERGE)


def _peer_layer(x1, x1t, x1t_bf, peer_w_q, peer_sub_keys, peer_u, peer_v, ln2_g, ln2_b):
    bf = jnp.bfloat16
    wq_hi, wq_lo = _split_bf16(peer_w_q)
    q = _matmul3(x1, wq_hi, wq_lo, *TILE_QPROJ)
    k_hi, k_lo = _split_bf16(peer_sub_keys)
    ry, cnt, amp, eb = _peer_prep(q, k_hi, k_lo, TILE_PREP)
    pre = _matmul(peer_u.astype(bf), x1t_bf, *TILE_PREACT, "peer_preact", bf)
    return _peer_main(pre, x1t, peer_v.T.astype(bf), ry, cnt, amp, eb, ln2_g, ln2_b, *TILE_PEER)


def kernel(x, w_in, da_lambda, da_subln_g, ml_conv_w, ml_conv_b, ml_gate_b, ml_norm_g,
           w_branch_attn, w_branch_mlstm, w_out, ln1_g, ln1_b,
           peer_w_q, peer_sub_keys, peer_u, peer_v, ln2_g, ln2_b):
    batch, seq, d = x.shape
    xt = x.reshape(batch * seq, d)
    x1, x1t, x1t_bf = _token_mixing(xt, batch, seq, w_in[0], da_lambda[0], da_subln_g[0], ml_conv_w[0],
                                    ml_conv_b[0], ml_gate_b[0], ml_norm_g[0], w_branch_attn[0],
                                    w_branch_mlstm[0], w_out[0], ln1_g[0], ln1_b[0])
    out = _peer_layer(x1, x1t, x1t_bf, peer_w_q[0], peer_sub_keys[0], peer_u[0], peer_v[0],
                      ln2_g[0], ln2_b[0])
    return out.reshape(batch, seq, d)
```

```python
import functools
import math

import jax
import jax.numpy as jnp
from jax import lax
from jax.experimental import pallas as pl
from jax.experimental.pallas import tpu as pltpu

D_MODEL = 1024
DA_QK_DIM = 64
DA_V_DIM = 128
DA_HEADS = 4
DA_WIDTH = 512
ROPE_DIM = 16
ROPE_THETA = 500000.0
ML_WIDTH = 512
ML_HEADS = 4
ML_HEAD_DIM = 128
ML_CONV = 5
ML_CHUNK = 128
STAB_INIT = -1e30
N_KEYS = 128
N_EXPERTS = N_KEYS * N_KEYS
PEER_HEADS = 8
PEER_TOPK = 16
DEPTH = 1
DN_ALPHA = (2 * DEPTH) ** 0.25
LN_EPS = 1e-5
LAMBDA_INIT = 0.8 - 0.6 * math.exp(-0.3 * 0)

LANES = 128
NEG_INF = float("-inf")
VMEM_LIMIT = 48 * 1024 * 1024

TILE_PROJ = (1024, 1920)
TILE_ROWS = 2048
TILE_ATTN = (256, 4096)
MLSTM_CHUNKS = 4
TILE_MERGE = 512
TILE_QPROJ = (1024, 1024)
TILE_PREP = 256
TILE_PREACT = (1024, 1024)
TILE_PEER = (512, 2048)

COL_BR = 0
COL_Q, COL_K, COL_V = 2048, 2560, 3072
COL_MLQ, COL_MLK, COL_MLV, COL_MLO = 3584, 4096, 4608, 5120
COL_MLG = 5632
N_GATE_COLS = 4 * ML_HEADS
IN_MAIN = 3584


def _params(*sem):
    return pltpu.CompilerParams(dimension_semantics=sem, vmem_limit_bytes=VMEM_LIMIT)


def _mm_kernel(x_ref, w_ref, o_ref):
    o_ref[...] = jnp.dot(x_ref[...].astype(jnp.bfloat16), w_ref[...],
                         preferred_element_type=jnp.float32).astype(o_ref.dtype)


def _matmul(x, w, tm, tn, name, out_dtype=jnp.float32):
    m, k = x.shape
    n = w.shape[1]
    tm, tn = min(tm, m), min(tn, n)
    return pl.pallas_call(
        _mm_kernel, grid=(m // tm, n // tn),
        in_specs=[pl.BlockSpec((tm, k), lambda i, j: (i, 0)),
                  pl.BlockSpec((k, tn), lambda i, j: (0, j))],
        out_specs=pl.BlockSpec((tm, tn), lambda i, j: (i, j)),
        out_shape=jax.ShapeDtypeStruct((m, n), out_dtype),
        compiler_params=_params("parallel", "parallel"), name=name)(x, w)


def _split_bf16(a):
    hi = a.astype(jnp.bfloat16)
    lo = (a - hi.astype(jnp.float32)).astype(jnp.bfloat16)
    return hi, lo


def _dot3(a_hi, a_lo, b_hi, b_lo, dims):
    f = lambda a, b: lax.dot_general(a, b, (dims, ((), ())), preferred_element_type=jnp.float32)
    return f(a_hi, b_hi) + f(a_hi, b_lo) + f(a_lo, b_hi)


def _mm3_kernel(x_ref, wh_ref, wl_ref, o_ref):
    xh, xl = _split_bf16(x_ref[...])
    o_ref[...] = _dot3(xh, xl, wh_ref[...], wl_ref[...], ((1,), (0,)))


def _matmul3(x, w_hi, w_lo, tm, tn):
    m, k = x.shape
    n = w_hi.shape[1]
    tm, tn = min(tm, m), min(tn, n)
    return pl.pallas_call(
        _mm3_kernel, grid=(m // tm, n // tn),
        in_specs=[pl.BlockSpec((tm, k), lambda i, j: (i, 0)),
                  pl.BlockSpec((k, tn), lambda i, j: (0, j)),
                  pl.BlockSpec((k, tn), lambda i, j: (0, j))],
        out_specs=pl.BlockSpec((tm, tn), lambda i, j: (i, j)),
        out_shape=jax.ShapeDtypeStruct((m, n), jnp.float32),
        compiler_params=_params("parallel", "parallel"), name="dense_matmul3")(x, w_hi, w_lo)


def _rope_kernel(q_ref, k_ref, v_ref, c_ref, s1_ref, s2_ref, o_ref):
    c, s1, s2 = c_ref[...], s1_ref[...], s2_ref[...]
    for which, src in enumerate((q_ref, k_ref)):
        for hd in range(DA_HEADS):
            x = src[:, hd * LANES:(hd + 1) * LANES]
            r = x * c + pltpu.roll(x, LANES - 8, 1) * s1 + pltpu.roll(x, 8, 1) * s2
            if which == 0:
                r = r * (DA_QK_DIM ** -0.5)
            col = which * DA_WIDTH + hd * LANES
            o_ref[:, col:col + LANES] = r.astype(jnp.bfloat16)
    o_ref[:, 2 * DA_WIDTH:3 * DA_WIDTH] = v_ref[...].astype(jnp.bfloat16)


def _rope_tables(seq):
    pos = jnp.arange(seq, dtype=jnp.float32)
    inv_freq = ROPE_THETA ** (-jnp.arange(0, ROPE_DIM, 2, dtype=jnp.float32) / ROPE_DIM)
    ang = pos[:, None] * inv_freq[None, :]
    cos, sin = jnp.cos(ang), jnp.sin(ang)
    half = ROPE_DIM // 2
    zeros = jnp.zeros((seq, DA_QK_DIM - ROPE_DIM), jnp.float32)
    z8 = jnp.zeros((seq, half), jnp.float32)
    c64 = jnp.concatenate([cos, cos, zeros + 1.0], -1)
    s1_64 = jnp.concatenate([-sin, z8, zeros], -1)
    s2_64 = jnp.concatenate([z8, sin, zeros], -1)
    tile2 = lambda t: jnp.concatenate([t, t], -1)
    return tile2(c64), tile2(s1_64), tile2(s2_64)


def _rope_cast(proj, seq, tm):
    t = proj.shape[0]
    tm = min(tm, seq)
    nsb = seq // tm
    c, s1, s2 = _rope_tables(seq)
    tab = pl.BlockSpec((tm, LANES), lambda i: (i % nsb, 0))
    src = lambda col: pl.BlockSpec((tm, DA_WIDTH), lambda i: (i, col // DA_WIDTH))
    return pl.pallas_call(
        _rope_kernel, grid=(t // tm,),
        in_specs=[src(COL_Q), src(COL_K), src(COL_V), tab, tab, tab],
        out_specs=pl.BlockSpec((tm, 3 * DA_WIDTH), lambda i: (i, 0)),
        out_shape=jax.ShapeDtypeStruct((t, 3 * DA_WIDTH), jnp.bfloat16),
        compiler_params=_params("parallel"), name="rope_cast")(proj, proj, proj, c, s1, s2)


def _attn_kernel(lam_ref, g_ref, q_ref, k_ref, v_ref, o_ref, m_ref, acc_ref, sa_ref, sb_ref, *, tk):
    nk = k_ref.shape[0] // tk
    m_ref[...] = jnp.full(m_ref.shape, NEG_INF, jnp.float32)
    acc_ref[...] = jnp.zeros(acc_ref.shape, jnp.float32)
    lane = lax.broadcasted_iota(jnp.int32, (tk, LANES), 1)
    ones_col = jnp.where(lane == 0, 1.0, 0.0).astype(jnp.bfloat16)

    def scores(j, s_ref):
        k = k_ref[pl.ds(pl.multiple_of(j * tk, tk), tk), :]
        for mp in range(2):
            q = q_ref[:, mp * DA_QK_DIM:(mp + 1) * DA_QK_DIM]
            s_ref[mp] = lax.dot_general(q, k[:, mp * DA_QK_DIM:(mp + 1) * DA_QK_DIM],
                                        (((1,), (1,)), ((), ())), preferred_element_type=jnp.float32)

    def accumulate(j, s_ref):
        vext = jnp.concatenate([v_ref[pl.ds(pl.multiple_of(j * tk, tk), tk), :], ones_col], axis=1)
        for mp in range(2):
            s = s_ref[mp]
            m_old = m_ref[mp]
            m_new = jnp.maximum(m_old, jnp.max(s, axis=-1, keepdims=True))
            p = jnp.exp(s - m_new).astype(jnp.bfloat16)
            pv = jnp.dot(p, vext, preferred_element_type=jnp.float32)
            acc_ref[mp] = jnp.exp(m_old - m_new) * acc_ref[mp] + pv
            m_ref[mp] = m_new

    scores(0, sa_ref)

    def body(jj, carry):
        j = 2 * jj
        scores(j + 1, sb_ref)
        accumulate(j, sa_ref)
        scores(j + 2, sa_ref)
        accumulate(j + 1, sb_ref)
        return carry

    lax.fori_loop(0, nk // 2 - 1, body, 0)
    scores(nk - 1, sb_ref)
    accumulate(nk - 2, sa_ref)
    accumulate(nk - 1, sb_ref)

    lp = lam_ref[...]
    lam = (jnp.exp(jnp.sum(lp[0:1] * lp[1:2], axis=-1, keepdims=True))
           - jnp.exp(jnp.sum(lp[2:3] * lp[3:4], axis=-1, keepdims=True)) + LAMBDA_INIT)
    a0, a1 = acc_ref[0], acc_ref[1]
    o = (a0[:, :DA_V_DIM] / a0[:, DA_V_DIM:DA_V_DIM + 1]
         - lam * (a1[:, :DA_V_DIM] / a1[:, DA_V_DIM:DA_V_DIM + 1]))
    y = o * lax.rsqrt(jnp.mean(o * o, axis=-1, keepdims=True) + LN_EPS)
    o_ref[...] = y * g_ref[...] * (1.0 - LAMBDA_INIT)


def _diff_attention(qkv, da_lambda, subln_g, batch, seq, tq, tk):
    t = qkv.shape[0]
    tq, tk = min(tq, seq), min(tk, seq // 2)
    assert (seq // tk) % 2 == 0, "the key loop is software-pipelined over chunk pairs"
    nq = seq // tq
    kern = functools.partial(_attn_kernel, tk=tk)
    return pl.pallas_call(
        kern, grid=(batch, DA_HEADS, nq),
        in_specs=[pl.BlockSpec((4, DA_QK_DIM), lambda b, h, i: (0, 0)),
                  pl.BlockSpec((1, DA_V_DIM), lambda b, h, i: (0, 0)),
                  pl.BlockSpec((tq, LANES), lambda b, h, i: (b * nq + i, h)),
                  pl.BlockSpec((seq, LANES), lambda b, h, i: (b, DA_HEADS + h)),
                  pl.BlockSpec((seq, LANES), lambda b, h, i: (b, 2 * DA_HEADS + h))],
        out_specs=pl.BlockSpec((tq, LANES), lambda b, h, i: (b * nq + i, h)),
        out_shape=jax.ShapeDtypeStruct((t, DA_WIDTH), jnp.float32),
        scratch_shapes=[pltpu.VMEM((2, tq, 1), jnp.float32),
                        pltpu.VMEM((2, tq, 2 * LANES), jnp.float32),
                        pltpu.VMEM((2, tq, tk), jnp.float32),
                        pltpu.VMEM((2, tq, tk), jnp.float32)],
        compiler_params=_params("parallel", "parallel", "parallel"),
        name="diff_attention")(da_lambda, subln_g.reshape(1, DA_V_DIM), qkv, qkv, qkv)


def _conv_kernel(prev_ref, cur_ref, next_ref, w_ref, b_ref, o_ref, *, nsb):
    i, c = pl.program_id(0), pl.program_id(1)
    tm = cur_ref.shape[0]
    sb = i % nsb
    prev = jnp.where(sb == 0, 0.0, prev_ref[...])
    nxt = jnp.where(sb == nsb - 1, 0.0, next_ref[...])
    ext = jnp.concatenate([prev, cur_ref[...], nxt], axis=0)
    n = tm + 16
    pad = (ML_CONV - 1) // 2
    acc = jnp.zeros((tm, cur_ref.shape[1]), jnp.float32) + b_ref[...]
    for w in range(ML_CONV):
        shift = (pad - w) % n
        z = ext if shift == 0 else pltpu.roll(ext, shift, 0)
        acc = acc + z[8:8 + tm] * w_ref[w:w + 1, :]
    y = acc * jax.nn.sigmoid(acc)
    y = y * jnp.where(c == 1, ML_HEAD_DIM ** -0.5, 1.0)
    o_ref[...] = y.astype(jnp.bfloat16)


def _conv_silu(proj, conv_w, conv_b, seq, tm):
    t = proj.shape[0]
    tm = min(tm, seq)
    nsb = seq // tm
    r8 = tm // 8
    nrow8 = t // 8
    cb0 = COL_MLQ // ML_WIDTH
    kern = functools.partial(_conv_kernel, nsb=nsb)
    return pl.pallas_call(
        kern, grid=(t // tm, 2),
        in_specs=[pl.BlockSpec((8, ML_WIDTH), lambda i, c: (jnp.maximum(i * r8 - 1, 0), cb0 + c)),
                  pl.BlockSpec((tm, ML_WIDTH), lambda i, c: (i, cb0 + c)),
                  pl.BlockSpec((8, ML_WIDTH), lambda i, c: (jnp.minimum((i + 1) * r8, nrow8 - 1), cb0 + c)),
                  pl.BlockSpec((ML_CONV, ML_WIDTH), lambda i, c: (0, c)),
                  pl.BlockSpec((1, ML_WIDTH), lambda i, c: (0, c))],
        out_specs=pl.BlockSpec((tm, ML_WIDTH), lambda i, c: (i, c)),
        out_shape=jax.ShapeDtypeStruct((t, 2 * ML_WIDTH), jnp.bfloat16),
        compiler_params=_params("parallel", "parallel"),
        name="conv_silu")(proj, proj, proj, conv_w, conv_b.reshape(1, 2 * ML_WIDTH))


def _seg_scan(x, pos, seq, op, fill, reverse):
    y = x
    d = 1
    while d < ML_CHUNK:
        if reverse:
            y = op(y, jnp.where(pos < ML_CHUNK - d, pltpu.roll(y, seq - d, 1), fill))
        else:
            y = op(y, jnp.where(pos >= d, pltpu.roll(y, d, 1), fill))
        d *= 2
    return y


def _gate_dir(i_pre, logf, pos, lane, seq, reverse):
    bj = _seg_scan(logf, pos, seq, jnp.add, 0.0, reverse)
    last = (pos == 0) if reverse else (pos == ML_CHUNK - 1)
    b_tot = _seg_scan(jnp.where(last, bj, NEG_INF), pos, seq, jnp.maximum, NEG_INF, not reverse)
    w = i_pre - bj
    cmax_w = _seg_scan(w, pos, seq, jnp.maximum, NEG_INF, reverse)
    g = b_tot - bj + i_pre
    g_max = jnp.maximum(_seg_scan(g, pos, seq, jnp.maximum, NEG_INF, False),
                        _seg_scan(g, pos, seq, jnp.maximum, NEG_INF, True))
    acc_a, acc_g = b_tot, g_max
    d = ML_CHUNK
    while d < seq:
        if reverse:
            ok = lane < seq - d
            pa = jnp.where(ok, pltpu.roll(acc_a, seq - d, 1), 0.0)
            pg = jnp.where(ok, pltpu.roll(acc_g, seq - d, 1), NEG_INF)
        else:
            ok = lane >= d
            pa = jnp.where(ok, pltpu.roll(acc_a, d, 1), 0.0)
            pg = jnp.where(ok, pltpu.roll(acc_g, d, 1), NEG_INF)
        acc_g = jnp.maximum(pg + acc_a, acc_g)
        acc_a = pa + acc_a
        d *= 2
    m_after = jnp.maximum(acc_a + STAB_INIT, acc_g)
    if reverse:
        m_before = jnp.where(lane < seq - ML_CHUNK, pltpu.roll(m_after, seq - ML_CHUNK, 1), STAB_INIT)
    else:
        m_before = jnp.where(lane >= ML_CHUNK, pltpu.roll(m_after, ML_CHUNK, 1), STAB_INIT)
    mr_rel = jnp.maximum(m_before, cmax_w)
    m_row = bj + mr_rel
    u = -mr_rel
    w_inter = jnp.exp(m_before - mr_rel)
    e_negm = jnp.exp(-m_row)
    wk = jnp.exp(g - m_after)
    a = jnp.exp(b_tot + m_before - m_after)
    return u, w, w_inter, e_negm, wk, a


def _gate_kernel(g_ref, b_ref, o_ref):
    seq = g_ref.shape[2]
    g = g_ref[0] + b_ref[...]
    i_pre, f_pre = g[0:8], g[8:16]
    logf = jnp.minimum(f_pre, 0.0) - jnp.log1p(jnp.exp(-jnp.abs(f_pre)))
    lane = lax.broadcasted_iota(jnp.int32, (2 * ML_HEADS, seq), 1)
    row = lax.broadcasted_iota(jnp.int32, (2 * ML_HEADS, seq), 0)
    pos = lane % ML_CHUNK
    fwd = _gate_dir(i_pre, logf, pos, lane, seq, False)
    bwd = _gate_dir(i_pre, logf, pos, lane, seq, True)
    for q, (af, ab) in enumerate(zip(fwd, bwd)):
        o_ref[0, q] = jnp.where(row < ML_HEADS, af, ab)


def _gate_scans(gates_t, gate_b):
    b, _, seq = gates_t.shape
    return pl.pallas_call(
        _gate_kernel, grid=(b,),
        in_specs=[pl.BlockSpec((1, 16, seq), lambda i: (i, 0, 0)),
                  pl.BlockSpec((16, 1), lambda i: (0, 0))],
        out_specs=pl.BlockSpec((1, 6, 8, seq), lambda i: (i, 0, 0, 0)),
        out_shape=jax.ShapeDtypeStruct((b, 6, 8, seq), jnp.float32),
        compiler_params=_params("parallel"), name="mlstm_gate_scans")(gates_t, gate_b.reshape(16, 1))


Q_U, Q_W, Q_WINTER, Q_ENEGM, Q_WK, Q_A = range(6)


def _mlstm_kernel(qf_ref, kf_ref, vf_ref, gf_ref, qb_ref, kb_ref, vb_ref, gb_ref,
                  of_ref, ob_ref, ce_ref, *, chunks):
    @pl.when(pl.program_id(1) == 0)
    def _():
        ce_ref[...] = jnp.zeros(ce_ref.shape, jnp.float32)

    L = ML_CHUNK
    row = lax.broadcasted_iota(jnp.int32, (L, L), 0)
    col = lax.broadcasted_iota(jnp.int32, (L, L), 1)
    lane = lax.broadcasted_iota(jnp.int32, (L, LANES), 1)
    ones_col = jnp.where(lane == 0, 1.0, 0.0).astype(jnp.bfloat16)
    dirs = ((0, qf_ref, kf_ref, vf_ref, gf_ref, of_ref, col <= row, range(chunks)),
            (1, qb_ref, kb_ref, vb_ref, gb_ref, ob_ref, col >= row, range(chunks - 1, -1, -1)))
    for d, q_ref, k_ref, v_ref, g_ref, o_ref, mask, order in dirs:
        n_q, n_r, rows = g_ref.shape[1:]
        cols = g_ref[0].reshape(n_q * n_r, rows).T
        for c in order:
            r0 = c * L
            for h in range(ML_HEADS):
                hs = slice(h * ML_HEAD_DIM, (h + 1) * ML_HEAD_DIM)
                cidx = lambda qn: cols[r0:r0 + L, qn * 8 + d * 4 + h:qn * 8 + d * 4 + h + 1]
                q = q_ref[r0:r0 + L, hs]
                k = k_ref[r0:r0 + L, hs]
                vext = jnp.concatenate([v_ref[r0:r0 + L, hs].astype(jnp.bfloat16), ones_col], axis=1)
                s = lax.dot_general(q, k, (((1,), (1,)), ((), ())), preferred_element_type=jnp.float32)
                w_row = g_ref[0, Q_W, d * 4 + h:d * 4 + h + 1, r0:r0 + L]
                dec = jnp.where(mask, jnp.exp(cidx(Q_U) + w_row), 0.0)
                sd = (s * dec).astype(jnp.bfloat16)
                ce = ce_ref[d * 4 + h]
                inter = jnp.dot(q, ce.astype(jnp.bfloat16), preferred_element_type=jnp.float32)
                intra = jnp.dot(sd, vext, preferred_element_type=jnp.float32)
                tot = cidx(Q_WINTER) * inter + intra
                den = jnp.maximum(jnp.abs(tot[:, ML_HEAD_DIM:ML_HEAD_DIM + 1]), cidx(Q_ENEGM))
                o_ref[r0:r0 + L, hs] = tot[:, :ML_HEAD_DIM] / den
                kw_t = (k.astype(jnp.float32) * cidx(Q_WK)).T.astype(jnp.bfloat16)
                upd = jnp.dot(kw_t, vext, preferred_element_type=jnp.float32)
                a = cols[r0:r0 + 1, Q_A * 8 + d * 4 + h:Q_A * 8 + d * 4 + h + 1]
                ce_ref[d * 4 + h] = a * ce + upd


def _mlstm_scan(qk_conv, proj, gtab, batch, seq, chunks):
    t = qk_conv.shape[0]
    rows = chunks * ML_CHUNK
    ns = seq // rows
    vcb = COL_MLV // ML_WIDTH
    fw = lambda b, j: b * ns + j
    bw = lambda b, j: b * ns + ns - 1 - j
    def specs(rb, lb):
        return [pl.BlockSpec((rows, ML_WIDTH), lambda b, j: (rb(b, j), 0)),
                pl.BlockSpec((rows, ML_WIDTH), lambda b, j: (rb(b, j), 1)),
                pl.BlockSpec((rows, ML_WIDTH), lambda b, j: (rb(b, j), vcb)),
                pl.BlockSpec((1,) + gtab.shape[1:3] + (rows,), lambda b, j: (b, 0, 0, lb(j)))]
    out_f = pl.BlockSpec((rows, ML_WIDTH), lambda b, j: (fw(b, j), 0))
    out_b = pl.BlockSpec((rows, ML_WIDTH), lambda b, j: (bw(b, j), 0))
    kern = functools.partial(_mlstm_kernel, chunks=chunks)
    return pl.pallas_call(
        kern, grid=(batch, ns),
        in_specs=specs(fw, lambda j: j) + specs(bw, lambda j: ns - 1 - j),
        out_specs=[out_f, out_b],
        out_shape=[jax.ShapeDtypeStruct((t, ML_WIDTH), jnp.float32)] * 2,
        scratch_shapes=[pltpu.VMEM((2 * ML_HEADS, ML_HEAD_DIM, 2 * LANES), jnp.float32)],
        compiler_params=_params("parallel", "arbitrary"),
        name="mlstm_scan")(qk_conv, qk_conv, proj, gtab, qk_conv, qk_conv, proj, gtab)


def _layer_norm_rows(z, g, b):
    mu = jnp.mean(z, axis=-1, keepdims=True)
    zc = z - mu
    var = jnp.mean(zc * zc, axis=-1, keepdims=True)
    return zc * lax.rsqrt(var + LN_EPS) * g + b


def _merge_kernel(x_ref, ya_ref, hf_ref, hb_ref, mo_ref, g0_ref, g1_ref, ng_ref,
                  wa_ref, wm_ref, wo_ref, lg_ref, lb_ref, o_ref, ot_ref, otb_ref):
    h = hf_ref[...] + hb_ref[...]
    parts = []
    for hd in range(ML_HEADS):
        hh = h[:, hd * ML_HEAD_DIM:(hd + 1) * ML_HEAD_DIM]
        mu = jnp.mean(hh, axis=-1, keepdims=True)
        hc = hh - mu
        var = jnp.mean(hc * hc, axis=-1, keepdims=True)
        parts.append(hc * lax.rsqrt(var + LN_EPS))
    ym = jnp.concatenate(parts, axis=1) * ng_ref[...] * jax.nn.sigmoid(mo_ref[...])
    pa = jnp.dot(ya_ref[...].astype(jnp.bfloat16), wa_ref[...], preferred_element_type=jnp.float32)
    pm = jnp.dot(ym.astype(jnp.bfloat16), wm_ref[...], preferred_element_type=jnp.float32)
    merged = jax.nn.sigmoid(g0_ref[...]) * pa + jax.nn.sigmoid(g1_ref[...]) * pm
    mix = jnp.dot(merged.astype(jnp.bfloat16), wo_ref[...], preferred_element_type=jnp.float32)
    x1 = _layer_norm_rows(DN_ALPHA * x_ref[...] + mix, lg_ref[...], lb_ref[...])
    o_ref[...] = x1
    x1t = x1.T
    ot_ref[...] = x1t
    otb_ref[...] = x1t.astype(jnp.bfloat16)


def _merge(xt, y_attn, h_fw, h_bw, proj, norm_g, wa, wm, wo, ln_g, ln_b, tm):
    t = xt.shape[0]
    tm = min(tm, t)
    row = lambda w, cb: pl.BlockSpec((tm, w), lambda i: (i, cb))
    full = lambda a: pl.BlockSpec(a.shape, lambda i: (0, 0))
    vec = lambda a: a.reshape(1, -1)
    args = (xt, y_attn, h_fw, h_bw, proj, proj, proj, vec(norm_g), wa, wm, wo, vec(ln_g), vec(ln_b))
    in_specs = [row(D_MODEL, 0), row(DA_WIDTH, 0), row(ML_WIDTH, 0), row(ML_WIDTH, 0),
                row(ML_WIDTH, COL_MLO // ML_WIDTH),
                row(D_MODEL, COL_BR // D_MODEL), row(D_MODEL, COL_BR // D_MODEL + 1)]
    in_specs += [full(a) for a in args[7:]]
    return pl.pallas_call(
        _merge_kernel, grid=(t // tm,), in_specs=in_specs,
        out_specs=[pl.BlockSpec((tm, D_MODEL), lambda i: (i, 0)),
                   pl.BlockSpec((D_MODEL, tm), lambda i: (0, i)),
                   pl.BlockSpec((D_MODEL, tm), lambda i: (0, i))],
        out_shape=[jax.ShapeDtypeStruct((t, D_MODEL), jnp.float32),
                   jax.ShapeDtypeStruct((D_MODEL, t), jnp.float32),
                   jax.ShapeDtypeStruct((D_MODEL, t), jnp.bfloat16)],
        compiler_params=_params("parallel"), name="merge_ln")(*args)


NOT_TOP = float(PEER_TOPK)
N_CAND_ROWS = 80


def _topk_rows(s, dst_ref, want_rank):
    rank = jnp.full(s.shape, NOT_TOP, jnp.float32) if want_rank else None
    for x in range(PEER_TOPK):
        m = jnp.max(s, axis=0, keepdims=True)
        dst_ref[x:x + 1, :] = m
        hit = s == m
        if want_rank:
            rank = jnp.where(hit, float(x), rank)
        s = jnp.where(hit, NEG_INF, s)
    return rank


def _peer_prep_kernel(q_ref, kh_ref, kl_ref, ry_ref, cnt_ref, amp_ref, eb_ref, a_ref, b_ref, sum_ref):
    for head in range(PEER_HEADS):
        _peer_prep_head(head, q_ref, kh_ref, kl_ref, ry_ref, cnt_ref, amp_ref, eb_ref, a_ref, b_ref, sum_ref)


def _peer_prep_head(head, q_ref, kh_ref, kl_ref, ry_ref, cnt_ref, amp_ref, eb_ref, a_ref, b_ref, sum_ref):
    half = N_KEYS
    sc = []
    for p in range(2):
        col = (2 * head + p) * half
        qh, ql = _split_bf16(q_ref[:, col:col + half])
        sc.append(_dot3(kh_ref[p], kl_ref[p], qh, ql, ((1,), (1,))))
    s1, s2 = sc
    _topk_rows(s1, a_ref, False)
    rank2 = _topk_rows(s2, b_ref, True)
    a, b = a_ref[...], b_ref[...]
    h8 = PEER_TOPK // 2
    sum_ref[0:PEER_TOPK, :] = a + b[0:1, :]
    for y in range(1, h8):
        sum_ref[PEER_TOPK + (y - 1) * h8:PEER_TOPK + y * h8, :] = a[0:h8, :] + b[y:y + 1, :]
    sum_ref[N_CAND_ROWS - h8:N_CAND_ROWS, :] = b[h8:PEER_TOPK, :] + a[0:1, :]
    sums = sum_ref[...]
    rest = sums
    for _ in range(PEER_TOPK):
        tau = jnp.max(rest, axis=0, keepdims=True)
        rest = jnp.where(rest == tau, NEG_INF, rest)
    top = a[0:1, :] + b[0:1, :]
    z = jnp.sum(jnp.where(sums >= tau, jnp.exp(sums - top), 0.0), axis=0, keepdims=True)
    cntx = jnp.zeros(a.shape, jnp.float32)
    for y in range(PEER_TOPK):
        cntx = cntx + jnp.where(a + b[y:y + 1, :] >= tau, 1.0, 0.0)
    cnt = jnp.zeros(s1.shape, jnp.float32)
    for x in range(PEER_TOPK):
        cnt = jnp.where(s1 == a[x:x + 1, :], cntx[x:x + 1, :], cnt)
    results = ((ry_ref, rank2.astype(jnp.bfloat16)),
               (cnt_ref, cnt),
               (amp_ref, 0.5 * jnp.exp(s1 - a[0:1, :])),
               (eb_ref, (jnp.exp(s2 - b[0:1, :]) / z).astype(jnp.bfloat16)))
    for ref, val in results:
        for tc in range(ref.shape[1]):
            ref[head, tc] = val[:, tc * LANES:(tc + 1) * LANES]


def _peer_prep(q, keys_hi, keys_lo, tt):
    t = q.shape[0]
    tt = min(tt, t)
    qdim = PEER_HEADS * 2 * N_KEYS
    shape = (PEER_HEADS, t // LANES, N_KEYS, LANES)
    outs = [jax.ShapeDtypeStruct(shape, dt) for dt in (jnp.bfloat16, jnp.float32, jnp.float32, jnp.bfloat16)]
    ospec = pl.BlockSpec((PEER_HEADS, tt // LANES, N_KEYS, LANES), lambda i: (0, i, 0, 0))
    kspec = pl.BlockSpec((2, N_KEYS, N_KEYS), lambda i: (0, 0, 0))
    return pl.pallas_call(
        _peer_prep_kernel, grid=(t // tt,),
        in_specs=[pl.BlockSpec((tt, qdim), lambda i: (i, 0)), kspec, kspec],
        out_specs=[ospec] * 4, out_shape=outs,
        scratch_shapes=[pltpu.VMEM((PEER_TOPK, tt), jnp.float32),
                        pltpu.VMEM((PEER_TOPK, tt), jnp.float32),
                        pltpu.VMEM((N_CAND_ROWS, tt), jnp.float32)],
        compiler_params=_params("parallel"), name="peer_prep")(q, keys_hi, keys_lo)


PEER_KEY_GROUP = 2


def _peer_main_kernel(pre_ref, xf_ref, vt_ref, ry_ref, cnt_ref, amp_ref, eb_ref, lg_ref, lb_ref,
                      o_ref, acc_ref, *, rows_per_step):
    e = pl.program_id(1)
    tb = pre_ref.shape[1]
    bf = jnp.bfloat16

    @pl.when(e == 0)
    def _():
        acc_ref[...] = jnp.zeros(acc_ref.shape, jnp.float32)

    def weigh(ii0):
        zero = jnp.zeros((N_KEYS, LANES), bf)
        out = [[] for _ in range(PEER_KEY_GROUP)]
        for tc in range(tb // LANES):
            cols = slice(tc * LANES, (tc + 1) * LANES)
            spread = lambda ref, h, a: jnp.broadcast_to(
                ref[h, tc, pl.ds(e * rows_per_step + ii0 + a, 1), :], (N_KEYS, LANES)).astype(bf)
            gs = [zero] * PEER_KEY_GROUP
            for h in range(PEER_HEADS):
                ry, eb = ry_ref[h, tc], eb_ref[h, tc]
                for a in range(PEER_KEY_GROUP):
                    sel = jnp.maximum(jnp.minimum(spread(cnt_ref, h, a) - ry, eb), zero)
                    gs[a] = gs[a] + spread(amp_ref, h, a) * sel
            for a in range(PEER_KEY_GROUP):
                rows = slice((ii0 + a) * N_KEYS, (ii0 + a + 1) * N_KEYS)
                pre = pre_ref[rows, cols].astype(jnp.float32)
                act = pre * (1.0 + lax.erf(pre * (2.0 ** -0.5)))
                out[a].append(gs[a] * act.astype(bf))
        return jnp.concatenate([jnp.concatenate(tiles, axis=1) for tiles in out], axis=0)

    wgt = jnp.concatenate([weigh(ii0) for ii0 in range(0, rows_per_step, PEER_KEY_GROUP)], axis=0)
    acc_ref[...] += lax.dot_general(vt_ref[...], wgt, (((0,), (0,)), ((), ())),
                                    preferred_element_type=jnp.float32)

    @pl.when(e == pl.num_programs(1) - 1)
    def _():
        z = DN_ALPHA * xf_ref[...] + acc_ref[...]
        mu = jnp.mean(z, axis=0, keepdims=True)
        zc = z - mu
        var = jnp.mean(zc * zc, axis=0, keepdims=True)
        o_ref[...] = (zc * lax.rsqrt(var + LN_EPS)).T * lg_ref[...] + lb_ref[...]


def _peer_main(pre, x1t, vt_bf, ry, cnt, amp, eb, ln_g, ln_b, tb, eb_rows):
    d, t = x1t.shape
    tb = min(tb, t)
    rows_per_step = eb_rows // N_KEYS
    tab = lambda: pl.BlockSpec((PEER_HEADS, tb // LANES, N_KEYS, LANES), lambda i, e: (0, i, 0, 0))
    colv = lambda: pl.BlockSpec((1, d), lambda i, e: (0, 0))
    kern = functools.partial(_peer_main_kernel, rows_per_step=rows_per_step)
    return pl.pallas_call(
        kern, grid=(t // tb, N_EXPERTS // eb_rows),
        in_specs=[pl.BlockSpec((eb_rows, tb), lambda i, e: (e, i)),
                  pl.BlockSpec((d, tb), lambda i, e: (0, i)),
                  pl.BlockSpec((eb_rows, d), lambda i, e: (e, 0)),
                  tab(), tab(), tab(), tab(), colv(), colv()],
        out_specs=pl.BlockSpec((tb, d), lambda i, e: (i, 0)),
        out_shape=jax.ShapeDtypeStruct((t, d), jnp.float32),
        scratch_shapes=[pltpu.VMEM((d, tb), jnp.float32)],
        compiler_params=_params("parallel", "arbitrary"),
        name="peer_dense")(pre, x1t, vt_bf, ry, cnt, amp, eb, ln_g.reshape(1, d), ln_b.reshape(1, d))


def _token_mixing(xt, batch, seq, w_in, da_lambda, da_subln_g, ml_conv_w, ml_conv_b, ml_gate_b,
                  ml_norm_g, w_branch_attn, w_branch_mlstm, w_out, ln1_g, ln1_b):
    bf = jnp.bfloat16
    zpad = jnp.zeros((D_MODEL, LANES - N_GATE_COLS), jnp.float32)
    w_cols = jnp.concatenate([w_in[:, IN_MAIN + N_GATE_COLS:], w_in[:, :IN_MAIN],
                              w_in[:, IN_MAIN:IN_MAIN + N_GATE_COLS], zpad], axis=1).astype(bf)
    proj = _matmul(xt, w_cols, *TILE_PROJ, "in_projection")

    qkv = _rope_cast(proj, seq, TILE_ROWS)
    y_attn = _diff_attention(qkv, da_lambda, da_subln_g, batch, seq, *TILE_ATTN)

    qk_conv = _conv_silu(proj, ml_conv_w, ml_conv_b, seq, TILE_ROWS)
    order = jnp.array([0, 2, 1, 3])
    gates_t = proj[:, COL_MLG:COL_MLG + N_GATE_COLS].reshape(batch, seq, 4, ML_HEADS)[:, :, order]
    gates_t = gates_t.reshape(batch, seq, N_GATE_COLS).transpose(0, 2, 1)
    gtab = _gate_scans(gates_t, ml_gate_b[order].reshape(N_GATE_COLS))
    h_fw, h_bw = _mlstm_scan(qk_conv, proj, gtab, batch, seq, min(MLSTM_CHUNKS, seq // ML_CHUNK))

    return _merge(xt, y_attn, h_fw, h_bw, proj, ml_norm_g,
                  w_branch_attn.astype(bf), w_branch_mlstm.astype(bf), w_out.astype(bf),
                  ln1_g, ln1_b, TILE_MERGE)


def _peer_layer(x1, x1t, x1t_bf, peer_w_q, peer_sub_keys, peer_u, peer_v, ln2_g, ln2_b):
    bf = jnp.bfloat16
    wq_hi, wq_lo = _split_bf16(peer_w_q)
    q = _matmul3(x1, wq_hi, wq_lo, *TILE_QPROJ)
    k_hi, k_lo = _split_bf16(peer_sub_keys)
    ry, cnt, amp, eb = _peer_prep(q, k_hi, k_lo, TILE_PREP)
    pre = _matmul(peer_u.astype(bf), x1t_bf, *TILE_PREACT, "peer_preact", bf)
    return _peer_main(pre, x1t, peer_v.astype(bf), ry, cnt, amp, eb, ln2_g, ln2_b, *TILE_PEER)


def kernel(x, w_in, da_lambda, da_subln_g, ml_conv_w, ml_conv_b, ml_gate_b, ml_norm_g,
           w_branch_attn, w_branch_mlstm, w_out, ln1_g, ln1_b,
           peer_w_q, peer_sub_keys, peer_u, peer_v, ln2_g, ln2_b):
    batch, seq, d = x.shape
    xt = x.reshape(batch * seq, d)
    x1, x1t, x1t_bf = _token_mixing(xt, batch, seq, w_in[0], da_lambda[0], da_subln_g[0], ml_conv_w[0],
                                    ml_conv_b[0], ml_gate_b[0], ml_norm_g[0], w_branch_attn[0],
                                    w_branch_mlstm[0], w_out[0], ln1_g[0], ln1_b[0])
    out = _peer_layer(x1, x1t, x1t_bf, peer_w_q[0], peer_sub_keys[0], peer_u[0], peer_v[0],
                      ln2_g[0], ln2_b[0])
    return out.reshape(batch, seq, d)
```
